```python
import math
import jax, jax.numpy as jnp
from jax import lax
import numpy as np

D_MODEL = 2048
BATCH = 2
SEQ = 4096
DEPTH = 2
DEC_BATCH = 128
DEC_SEQ = 8
PAST_LEN = 2048
PAGE_SIZE = 128

MIX_W = D_MODEL
ATTN_W = MIX_W // 2
POOL_W = MIX_W - ATTN_W
HEAD_DIM = 64
N_HEADS = ATTN_W // (2 * HEAD_DIM)
QK_W = 2 * HEAD_DIM
V_DIM = 2 * HEAD_DIM
POOL_WINDOWS = (2, 4, 8, 16)
N_POOL_GROUPS = len(POOL_WINDOWS)
POOL_GW = POOL_W // N_POOL_GROUPS
POOL_STATE = max(POOL_WINDOWS) - 1
IN_COLS = 4 * ATTN_W + 2 * POOL_W
ROPE_THETA = 10000.0
EPS = 1e-6
Q_BLOCK = 128
ATTN_SCALE = HEAD_DIM ** -0.5
NEG = -1e30

kernel_name = "hymba_diffattn_multiscale_pool_decode_step"


def rms_norm(x, g):
    xf = x.astype(jnp.float32)
    y = xf * lax.rsqrt(jnp.mean(xf * xf, axis=-1, keepdims=True) + EPS)
    return (y * g.astype(jnp.float32)).astype(x.dtype)


def rope(x, pos):
    half = HEAD_DIM // 2
    inv = ROPE_THETA ** (-jnp.arange(half, dtype=jnp.float32) * 2.0 / HEAD_DIM)
    ang = pos.astype(jnp.float32)[:, None] * inv[None, :]
    cos = jnp.cos(ang)[None, :, None, None, :]
    sin = jnp.sin(ang)[None, :, None, None, :]
    xf = x.astype(jnp.float32)
    x1, x2 = xf[..., :half], xf[..., half:]
    return jnp.concatenate([x1 * cos - x2 * sin, x2 * cos + x1 * sin], axis=-1).astype(x.dtype)


def lambda_init_of(layer):
    return 0.8 - 0.6 * math.exp(-0.3 * layer)


def diff_combine(s, lam, v, eq):
    p = jax.nn.softmax(s, axis=-1)
    a = jnp.take(p, 0, axis=-4) - lam * jnp.take(p, 1, axis=-4)
    return jnp.einsum(eq, a.astype(v.dtype), v)


def diff_attn_prompt(q, k, v, lam):
    B, S = q.shape[0], q.shape[1]
    nb = S // Q_BLOCK
    qb = q.reshape(B, nb, Q_BLOCK, N_HEADS, 2, HEAD_DIM).transpose(1, 0, 2, 3, 4, 5)
    kpos = jnp.arange(S)

    def block(args):
        qi, bi = args
        s = jnp.einsum('bqhcd,bkhcd->bchqk', qi, k, preferred_element_type=jnp.float32) * ATTN_SCALE
        qpos = bi * Q_BLOCK + jnp.arange(Q_BLOCK)
        s = jnp.where(kpos[None, :] <= qpos[:, None], s, NEG)
        return diff_combine(s, lam, v, 'bhqk,bkhv->bqhv')

    out = lax.map(block, (qb, jnp.arange(nb)))
    return out.transpose(1, 0, 2, 3, 4).reshape(B, S, N_HEADS, V_DIM)


def diff_attn_sample(q, k, v, cache_k, cache_v, page_table, layer, lam):
    T = q.shape[1]

    def one(args):
        qi, ki, vi, pages = args
        kp = cache_k[layer, pages].reshape(-1, N_HEADS, 2, HEAD_DIM).astype(ki.dtype)
        vp = cache_v[layer, pages].reshape(-1, N_HEADS, V_DIM).astype(vi.dtype)
        L = kp.shape[0]
        kall = jnp.concatenate([kp, ki], axis=0)
        vall = jnp.concatenate([vp, vi], axis=0)
        s = jnp.einsum('qhcd,khcd->chqk', qi, kall, preferred_element_type=jnp.float32) * ATTN_SCALE
        mask = jnp.arange(L + T)[None, :] <= (L + jnp.arange(T))[:, None]
        s = jnp.where(mask, s, NEG)
        return diff_combine(s, lam, vall, 'hqk,khv->qhv')

    return lax.map(one, (q, k, v, page_table))


def pool_mix(prefix, u, start, pool_w, pool_scale):
    B, T = u.shape[0], u.shape[1]
    full = jnp.concatenate([prefix.astype(u.dtype), u], axis=1)
    ff = full.astype(jnp.float32)
    cs = jnp.concatenate([jnp.zeros((B, 1, POOL_W), jnp.float32), jnp.cumsum(ff, axis=1)], axis=1)
    end = cs[:, POOL_STATE + 1:]
    pos = start + jnp.arange(T)
    means = []
    for g, w in enumerate(POOL_WINDOWS):
        sl = slice(g * POOL_GW, (g + 1) * POOL_GW)
        ssum = end[..., sl] - cs[:, POOL_STATE + 1 - w: POOL_STATE + 1 - w + T, sl]
        cnt = jnp.minimum(pos + 1, w).astype(jnp.float32)
        means.append(ssum / cnt[None, :, None])
    d = jnp.concatenate(means, axis=-1) - ff[:, POOL_STATE:]
    d = d.reshape(B, T, N_POOL_GROUPS, POOL_GW).astype(u.dtype)
    y = jnp.einsum('btgc,gcd->btgd', d, pool_w).reshape(B, T, POOL_W) * pool_scale
    return y, full[:, -POOL_STATE:]


def mixer_layer(x, start, pool_prefix, attend, layer, norm_pre, norm_post, w_in,
                lambda_q1, lambda_k1, lambda_q2, lambda_k2, subln_g, pool_w, pool_scale, w_out):
    B, T = x.shape[0], x.shape[1]
    lam_init = lambda_init_of(layer)
    lam = (jnp.exp(jnp.sum(lambda_q1[layer].astype(jnp.float32) * lambda_k1[layer].astype(jnp.float32)))
           - jnp.exp(jnp.sum(lambda_q2[layer].astype(jnp.float32) * lambda_k2[layer].astype(jnp.float32)))
           + lam_init)
    h = rms_norm(x, norm_pre[layer])
    z = h @ w_in[layer]
    q, k, v, ga, u, gp = jnp.split(z, [ATTN_W, 2 * ATTN_W, 3 * ATTN_W, 4 * ATTN_W, 4 * ATTN_W + POOL_W], axis=-1)
    pos = start + jnp.arange(T)
    q = rope(q.reshape(B, T, N_HEADS, 2, HEAD_DIM), pos)
    k = rope(k.reshape(B, T, N_HEADS, 2, HEAD_DIM), pos)
    v = v.reshape(B, T, N_HEADS, V_DIM)
    o = attend(q, k, v, lam)
    o = rms_norm(o, subln_g[layer]) * (1.0 - lam_init)
    o = o.reshape(B, T, ATTN_W) * jax.nn.silu(ga)
    p, pool_state = pool_mix(pool_prefix, u, start, pool_w[layer], pool_scale[layer])
    p = p * jax.nn.silu(gp)
    y = jnp.concatenate([o, p], axis=-1) @ w_out[layer]
    x = x + rms_norm(y, norm_post[layer])
    return x, k.reshape(B, T, N_HEADS, QK_W), v, pool_state


def setup_inputs(seed: int = 0) -> dict:
    key = jax.random.key(seed)
    ks = jax.random.split(key, 20)
    n_pages = PAST_LEN // PAGE_SIZE
    n_used = DEC_BATCH * n_pages
    n_phys = n_used + n_used // 4
    f32 = jnp.float32
    nrm = lambda k, s, sc: jax.random.normal(k, s, f32) * sc
    page_table = jax.random.permutation(ks[5], n_phys)[:n_used].reshape(DEC_BATCH, n_pages).astype(jnp.int32)
    return {
        "x_prompt": nrm(ks[0], (BATCH, SEQ, D_MODEL), 1.0),
        "x_sample": nrm(ks[1], (DEC_BATCH, DEC_SEQ, D_MODEL), 1.0),
        "cache_k": nrm(ks[2], (DEPTH, n_phys, PAGE_SIZE, N_HEADS, QK_W), 1.0),
        "cache_v": nrm(ks[3], (DEPTH, n_phys, PAGE_SIZE, N_HEADS, V_DIM), 1.0),
        "state_pool": nrm(ks[4], (DEPTH, DEC_BATCH, POOL_STATE, POOL_W), 1.0),
        "page_table": page_table,
        "norm_pre": 1.0 + nrm(ks[6], (DEPTH, D_MODEL), 0.05),
        "norm_post": 1.0 + nrm(ks[7], (DEPTH, D_MODEL), 0.05),
        "w_in": nrm(ks[8], (DEPTH, D_MODEL, IN_COLS), D_MODEL ** -0.5),
        "lambda_q1": nrm(ks[9], (DEPTH, HEAD_DIM), 0.1),
        "lambda_k1": nrm(ks[10], (DEPTH, HEAD_DIM), 0.1),
        "lambda_q2": nrm(ks[11], (DEPTH, HEAD_DIM), 0.1),
        "lambda_k2": nrm(ks[12], (DEPTH, HEAD_DIM), 0.1),
        "subln_g": 1.0 + nrm(ks[13], (DEPTH, V_DIM), 0.05),
        "pool_w": nrm(ks[14], (DEPTH, N_POOL_GROUPS, POOL_GW, POOL_GW), POOL_GW ** -0.5),
        "pool_scale": 1.0 + nrm(ks[15], (DEPTH, POOL_W), 0.1),
        "w_out": nrm(ks[16], (DEPTH, MIX_W, D_MODEL), MIX_W ** -0.5),
    }


def reference(x_prompt, x_sample, cache_k, cache_v, state_pool, page_table,
              norm_pre, norm_post, w_in, lambda_q1, lambda_k1, lambda_q2, lambda_k2,
              subln_g, pool_w, pool_scale, w_out):
    past_len = page_table.shape[1] * cache_k.shape[2]
    weights = (norm_pre, norm_post, w_in, lambda_q1, lambda_k1, lambda_q2, lambda_k2,
               subln_g, pool_w, pool_scale, w_out)
    xp, xs = x_prompt, x_sample
    kp_l, vp_l, pp_l, ks_l, vs_l, ps_l = [], [], [], [], [], []
    for layer in range(DEPTH):
        prefix0 = jnp.zeros((xp.shape[0], POOL_STATE, POOL_W), xp.dtype)
        xp, kp, vp, pp = mixer_layer(xp, 0, prefix0, diff_attn_prompt, layer, *weights)
        attend_s = lambda q, k, v, lam, layer=layer: diff_attn_sample(
            q, k, v, cache_k, cache_v, page_table, layer, lam)
        xs, ksn, vsn, psn = mixer_layer(xs, past_len, state_pool[layer], attend_s, layer, *weights)
        kp_l.append(kp); vp_l.append(vp); pp_l.append(pp)
        ks_l.append(ksn); vs_l.append(vsn); ps_l.append(psn)
    return (xp, xs, jnp.stack(kp_l), jnp.stack(vp_l), jnp.stack(pp_l),
            jnp.stack(ks_l), jnp.stack(vs_l), jnp.stack(ps_l))
```

```python
import functools
import math

import jax
import jax.numpy as jnp
from jax import lax
from jax.experimental import pallas as pl
from jax.experimental.pallas import tpu as pltpu

F32 = jnp.float32
BF16 = jnp.bfloat16

HEAD_DIM = 64
HEAD_W = 2 * HEAD_DIM
POOL_WINDOWS = (2, 4, 8, 16)
POOL_STATE = max(POOL_WINDOWS) - 1
POOL_PREV = POOL_STATE + 1
ROPE_THETA = 10000.0
EPS = 1e-6
NEG = -1e30
ATTN_SCALE = HEAD_DIM ** -0.5

LANES = 128
SUBLANES = 8
VMEM_LIMIT = 56 * 1024 * 1024


def _params(n_axes):
    return pltpu.CompilerParams(
        dimension_semantics=("arbitrary",) * n_axes, vmem_limit_bytes=VMEM_LIMIT)


def _lambda_init(layer):
    return 0.8 - 0.6 * math.exp(-0.3 * layer)


def _lam_from(lamv, lam_init):
    t1 = jnp.sum(lamv[0:1, :] * lamv[1:2, :], axis=-1, keepdims=True)
    t2 = jnp.sum(lamv[2:3, :] * lamv[3:4, :], axis=-1, keepdims=True)
    return jnp.exp(t1) - jnp.exp(t2) + lam_init


def _silu(x):
    return x * (1.0 / (1.0 + jnp.exp(-x)))


def _head_norm_gate(o, g, lam_init):
    ms = jnp.mean(o * o, axis=-1, keepdims=True)
    return (o * lax.rsqrt(ms + EPS) * g) * (1.0 - lam_init)


def _inproj_kernel(x_ref, g_ref, w_ref, cos_ref, sin_ref, z_ref, h_sc, *, n_rope_sections):
    j = pl.program_id(1)

    @pl.when(j == 0)
    def _():
        x = x_ref[...]
        ms = jnp.mean(x * x, axis=-1, keepdims=True)
        h_sc[...] = ((x * lax.rsqrt(ms + EPS)) * g_ref[...]).astype(BF16)

    acc = jnp.dot(h_sc[...], w_ref[...], preferred_element_type=F32)

    @pl.when(j >= n_rope_sections)
    def _():
        z_ref[...] = acc

    @pl.when(j < n_rope_sections)
    def _():
        cos = cos_ref[...]
        sin = sin_ref[...]
        lane = lax.broadcasted_iota(jnp.int32, cos.shape, 1)
        first_half = (lane % HEAD_DIM) < (HEAD_DIM // 2)
        scale = jnp.where(j == 0, ATTN_SCALE, 1.0).astype(F32)
        for hh in range(acc.shape[1] // LANES):
            blk = acc[:, hh * LANES:(hh + 1) * LANES]
            partner = jnp.where(first_half,
                                pltpu.roll(blk, LANES - HEAD_DIM // 2, 1),
                                pltpu.roll(blk, HEAD_DIM // 2, 1))
            z_ref[:, hh * LANES:(hh + 1) * LANES] = (blk * cos + partner * sin) * scale


def _inproj(x2d, g, w_bf16, cos_t, sin_t, *, tm, tn):
    T, D = x2d.shape
    n_cols = w_bf16.shape[1]
    period_blocks = cos_t.shape[0] // tm
    attn_w = n_cols // 6
    kern = functools.partial(_inproj_kernel, n_rope_sections=2 * attn_w // tn)
    return pl.pallas_call(
        kern,
        grid=(T // tm, n_cols // tn),
        in_specs=[
            pl.BlockSpec((tm, D), lambda i, j: (i, 0)),
            pl.BlockSpec((1, D), lambda i, j: (0, 0)),
            pl.BlockSpec((D, tn), lambda i, j: (0, j)),
            pl.BlockSpec((tm, LANES), lambda i, j: (i % period_blocks, 0)),
            pl.BlockSpec((tm, LANES), lambda i, j: (i % period_blocks, 0)),
        ],
        out_specs=pl.BlockSpec((tm, tn), lambda i, j: (i, j)),
        out_shape=jax.ShapeDtypeStruct((T, n_cols), F32),
        scratch_shapes=[pltpu.VMEM((tm, D), BF16)],
        compiler_params=_params(2),
        name="inproj",
    )(x2d, g, w_bf16, cos_t, sin_t)


def _attn_prompt_kernel(qmap_ref, kmap_ref, q_ref, k_ref, v_ref, ga_ref, lamv_ref, g_ref,
                        o_ref, m_sc, l_sc, acc_sc, *, lam_init):
    step = pl.program_id(2)
    qi = qmap_ref[step]
    ki = kmap_ref[step]

    @pl.when(ki == 0)
    def _():
        m_sc[...] = jnp.full(m_sc.shape, NEG, F32)
        l_sc[...] = jnp.zeros(l_sc.shape, F32)
        acc_sc[...] = jnp.zeros(acc_sc.shape, F32)

    def accumulate(masked):
        q = q_ref[...]
        k = k_ref[...].astype(BF16)
        v = v_ref[...].astype(BF16)
        lane = lax.broadcasted_iota(jnp.int32, q.shape, 1)
        for c in range(2):
            in_map = (lane >= c * HEAD_DIM) & (lane < (c + 1) * HEAD_DIM)
            qc = jnp.where(in_map, q, 0.0).astype(BF16)
            s = lax.dot_general(qc, k, (((1,), (1,)), ((), ())), preferred_element_type=F32)
            if masked:
                row = lax.broadcasted_iota(jnp.int32, s.shape, 0)
                col = lax.broadcasted_iota(jnp.int32, s.shape, 1)
                s = jnp.where(col <= row, s, NEG)
            m_prev = m_sc[c]
            m_new = jnp.maximum(m_prev, jnp.max(s, axis=-1, keepdims=True))
            alpha = jnp.exp(m_prev - m_new)
            p = jnp.exp(s - m_new)
            l_sc[c] = alpha * l_sc[c] + jnp.sum(p, axis=-1, keepdims=True)
            acc_sc[c] = alpha * acc_sc[c] + jnp.dot(p.astype(BF16), v, preferred_element_type=F32)
            m_sc[c] = m_new

    @pl.when(ki < qi)
    def _():
        accumulate(False)

    @pl.when(ki == qi)
    def _():
        accumulate(True)
        lam = _lam_from(lamv_ref[...], lam_init)
        o = acc_sc[0] / l_sc[0] - lam * (acc_sc[1] / l_sc[1])
        o = _head_norm_gate(o, g_ref[...], lam_init)
        o_ref[...] = (o * _silu(ga_ref[...])).astype(o_ref.dtype)


def _attn_prompt(z, lamv, subln_g, *, batch, seq, n_heads, lam_init, tq):
    nq = seq // tq
    pairs = [(qi, ki) for qi in range(nq) for ki in range(qi + 1)]
    qmap = jnp.asarray([p[0] for p in pairs], jnp.int32)
    kmap = jnp.asarray([p[1] for p in pairs], jnp.int32)
    attn_w = n_heads * HEAD_W
    kb = attn_w // LANES
    kern = functools.partial(_attn_prompt_kernel, lam_init=lam_init)
    grid_spec = pltpu.PrefetchScalarGridSpec(
        num_scalar_prefetch=2,
        grid=(batch, n_heads, len(pairs)),
        in_specs=[
            pl.BlockSpec((tq, LANES), lambda b, h, s, qm, km: (b * nq + qm[s], h)),
            pl.BlockSpec((tq, LANES), lambda b, h, s, qm, km: (b * nq + km[s], kb + h)),
            pl.BlockSpec((tq, LANES), lambda b, h, s, qm, km: (b * nq + km[s], 2 * kb + h)),
            pl.BlockSpec((tq, LANES), lambda b, h, s, qm, km: (b * nq + qm[s], 3 * kb + h)),
            pl.BlockSpec((4, HEAD_DIM), lambda b, h, s, qm, km: (0, 0)),
            pl.BlockSpec((1, HEAD_W), lambda b, h, s, qm, km: (0, 0)),
        ],
        out_specs=pl.BlockSpec((tq, LANES), lambda b, h, s, qm, km: (b * nq + qm[s], h)),
        scratch_shapes=[
            pltpu.VMEM((2, tq, 1), F32),
            pltpu.VMEM((2, tq, 1), F32),
            pltpu.VMEM((2, tq, HEAD_W), F32),
        ],
    )
    return pl.pallas_call(
        kern,
        grid_spec=grid_spec,
        out_shape=jax.ShapeDtypeStruct((batch * seq, attn_w), BF16),
        compiler_params=_params(3),
        name="attn_prompt",
    )(qmap, kmap, z, z, z, z, lamv, subln_g)


def _attn_decode_kernel(pt_ref, q_ref, kn_ref, vn_ref, ga_ref, lamv_ref, g_ref, *rest,
                        lam_init, n_pages, n_heads):
    del pt_ref
    k_pages = rest[:n_pages]
    v_pages = rest[n_pages:2 * n_pages]
    o_ref = rest[2 * n_pages]
    t_new = q_ref.shape[0]
    n_rows = 2 * n_heads * t_new

    q = q_ref[...]
    q_rep = jnp.concatenate([q] * (2 * n_heads), axis=0)
    row = lax.broadcasted_iota(jnp.int32, q_rep.shape, 0)
    col = lax.broadcasted_iota(jnp.int32, q_rep.shape, 1)
    q_bd = jnp.where((col // HEAD_DIM) == (row // t_new), q_rep, 0.0).astype(BF16)

    def gather_heads(page_ref):
        return jnp.concatenate([page_ref[:, h, :] for h in range(n_heads)], axis=1).astype(BF16)

    nt = (((1,), (1,)), ((), ()))
    s_past = jnp.concatenate(
        [lax.dot_general(q_bd, gather_heads(k_pages[j]), nt, preferred_element_type=F32)
         for j in range(n_pages)], axis=1)
    s_new = lax.dot_general(q_bd, kn_ref[...].astype(BF16), nt, preferred_element_type=F32)
    r_new = lax.broadcasted_iota(jnp.int32, s_new.shape, 0)
    c_new = lax.broadcasted_iota(jnp.int32, s_new.shape, 1)
    s_new = jnp.where(c_new <= (r_new % t_new), s_new, NEG)

    m = jnp.maximum(jnp.max(s_past, axis=-1, keepdims=True),
                    jnp.max(s_new, axis=-1, keepdims=True))
    p_past = jnp.exp(s_past - m)
    p_new = jnp.exp(s_new - m)
    inv_l = 1.0 / (jnp.sum(p_past, axis=-1, keepdims=True) + jnp.sum(p_new, axis=-1, keepdims=True))
    p_past = (p_past * inv_l).astype(BF16)
    p_new = (p_new * inv_l).astype(BF16)

    page = k_pages[0].shape[0]
    acc = jnp.dot(p_new, vn_ref[...].astype(BF16), preferred_element_type=F32)
    for j in range(n_pages):
        acc = acc + jnp.dot(p_past[:, j * page:(j + 1) * page], gather_heads(v_pages[j]),
                            preferred_element_type=F32)

    lam = _lam_from(lamv_ref[...], lam_init)
    g = g_ref[...]
    outs = []
    for h in range(n_heads):
        r0 = h * 2 * t_new
        o0 = acc[r0:r0 + t_new, h * HEAD_W:(h + 1) * HEAD_W]
        o1 = acc[r0 + t_new:r0 + 2 * t_new, h * HEAD_W:(h + 1) * HEAD_W]
        outs.append(_head_norm_gate(o0 - lam * o1, g, lam_init))
    o_ref[...] = jnp.concatenate(outs, axis=1) * _silu(ga_ref[...])


def _attn_decode(z, cache_k, cache_v, page_table, lamv, subln_g, *, layer, t_new, n_heads, lam_init):
    dec_batch, n_pages = page_table.shape
    page = cache_k.shape[2]
    attn_w = n_heads * HEAD_W
    row_spec = lambda sec: pl.BlockSpec((t_new, attn_w), lambda b, pt: (b, sec))
    page_specs = [
        pl.BlockSpec((None, None, page, n_heads, HEAD_W),
                     lambda b, pt, j=j: (layer, pt[b, j], 0, 0, 0))
        for j in range(n_pages)
    ]
    kern = functools.partial(_attn_decode_kernel, lam_init=lam_init, n_pages=n_pages, n_heads=n_heads)
    grid_spec = pltpu.PrefetchScalarGridSpec(
        num_scalar_prefetch=1,
        grid=(dec_batch,),
        in_specs=[row_spec(0), row_spec(1), row_spec(2), row_spec(3),
                  pl.BlockSpec((4, HEAD_DIM), lambda b, pt: (0, 0)),
                  pl.BlockSpec((1, HEAD_W), lambda b, pt: (0, 0))] + page_specs + page_specs,
        out_specs=pl.BlockSpec((t_new, attn_w), lambda b, pt: (b, 0)),
    )
    return pl.pallas_call(
        kern,
        grid_spec=grid_spec,
        out_shape=jax.ShapeDtypeStruct((dec_batch * t_new, attn_w), F32),
        compiler_params=_params(1),
        name="attn_decode",
    )(page_table, z, z, z, z, lamv, subln_g, *([cache_k] * n_pages), *([cache_v] * n_pages))


def _pool_kernel(u_ref, prev_ref, gp_ref, w_ref, sc_ref, p_ref, ext_sc, *, start, zero_first_prefix):
    i = pl.program_id(1)
    nb, tt, width = u_ref.shape
    gw = width // len(POOL_WINDOWS)
    prev = prev_ref[...]
    if zero_first_prefix:
        prev = jnp.where(i == 0, 0.0, prev)
    u = u_ref[...]
    ext_sc[:, 0:POOL_PREV, :] = prev
    ext_sc[:, POOL_PREV:POOL_PREV + tt, :] = u
    pos = start + i * tt + lax.broadcasted_iota(jnp.int32, (1, tt, 1), 1)
    for g, w in enumerate(POOL_WINDOWS):
        cols = slice(g * gw, (g + 1) * gw)
        ssum = u[:, :, cols]
        for back in range(1, w):
            ssum = ssum + ext_sc[:, POOL_PREV - back:POOL_PREV - back + tt, cols]
        cnt = jnp.minimum(pos + 1, w).astype(F32)
        d = (ssum / cnt - u[:, :, cols]).astype(BF16).reshape(nb * tt, gw)
        y = jnp.dot(d, w_ref[g], preferred_element_type=F32) * sc_ref[:, cols]
        y = y * _silu(gp_ref[:, :, cols].reshape(nb * tt, gw))
        p_ref[:, :, cols] = y.reshape(nb, tt, gw).astype(p_ref.dtype)


def _pool(z3, prev_src, pool_w_bf16, pool_scale, *, nb, tt, start, prev_from_z, out_dtype):
    n_seq, seq, n_cols = z3.shape
    width = n_cols // 6
    u_blk, gp_blk = 4, 5
    if prev_from_z:
        per = tt // POOL_PREV
        prev_spec = pl.BlockSpec((nb, POOL_PREV, width),
                                 lambda b, i: (b, jnp.maximum(i * per - 1, 0), u_blk))
    else:
        prev_spec = pl.BlockSpec((nb, POOL_PREV, width), lambda b, i: (b, 0, 0))
    kern = functools.partial(_pool_kernel, start=start, zero_first_prefix=prev_from_z)
    return pl.pallas_call(
        kern,
        grid=(n_seq // nb, seq // tt),
        in_specs=[
            pl.BlockSpec((nb, tt, width), lambda b, i: (b, i, u_blk)),
            prev_spec,
            pl.BlockSpec((nb, tt, width), lambda b, i: (b, i, gp_blk)),
            pl.BlockSpec(pool_w_bf16.shape, lambda b, i: (0, 0, 0)),
            pl.BlockSpec((1, width), lambda b, i: (0, 0)),
        ],
        out_specs=pl.BlockSpec((nb, tt, width), lambda b, i: (b, i, 0)),
        out_shape=jax.ShapeDtypeStruct((n_seq, seq, width), out_dtype),
        scratch_shapes=[pltpu.VMEM((nb, POOL_PREV + tt, width), F32)],
        compiler_params=_params(2),
        name="pool",
    )(z3, prev_src, z3, pool_w_bf16, pool_scale)


def _outproj_kernel(o_ref, p_ref, x_ref, wo_ref, wp_ref, g_ref, out_ref):
    y = (jnp.dot(o_ref[...].astype(BF16), wo_ref[...], preferred_element_type=F32)
         + jnp.dot(p_ref[...].astype(BF16), wp_ref[...], preferred_element_type=F32))
    ms = jnp.mean(y * y, axis=-1, keepdims=True)
    out_ref[...] = x_ref[...] + (y * lax.rsqrt(ms + EPS)) * g_ref[...]


def _outproj(o, p, x2d, w_o, w_p, g, *, tm):
    T, D = x2d.shape
    half = o.shape[1]
    return pl.pallas_call(
        _outproj_kernel,
        grid=(T // tm,),
        in_specs=[
            pl.BlockSpec((tm, half), lambda i: (i, 0)),
            pl.BlockSpec((tm, half), lambda i: (i, 0)),
            pl.BlockSpec((tm, D), lambda i: (i, 0)),
            pl.BlockSpec((half, D), lambda i: (0, 0)),
            pl.BlockSpec((half, D), lambda i: (0, 0)),
            pl.BlockSpec((1, D), lambda i: (0, 0)),
        ],
        out_specs=pl.BlockSpec((tm, D), lambda i: (i, 0)),
        out_shape=jax.ShapeDtypeStruct((T, D), F32),
        compiler_params=_params(1),
        name="outproj",
    )(o, p, x2d, w_o, w_p, g)


def _rope_tables(pos):
    half = HEAD_DIM // 2
    inv = ROPE_THETA ** (-jnp.arange(half, dtype=F32) * 2.0 / HEAD_DIM)
    ang = pos.astype(F32)[:, None] * inv[None, :]
    reps = LANES // half
    cos = jnp.tile(jnp.cos(ang), (1, reps))
    sin = jnp.tile(jnp.sin(ang), (1, reps))
    lane = jnp.arange(LANES)
    sign = jnp.where((lane % HEAD_DIM) < half, -1.0, 1.0).astype(F32)
    return cos, sin * sign[None, :]


def kernel(x_prompt, x_sample, cache_k, cache_v, state_pool, page_table, norm_pre, norm_post, w_in,
           lambda_q1, lambda_k1, lambda_q2, lambda_k2, subln_g, pool_w, pool_scale, w_out):
    batch, seq, d_model = x_prompt.shape
    dec_batch, dec_seq, _ = x_sample.shape
    depth = w_in.shape[0]
    n_heads = cache_k.shape[3]
    attn_w = n_heads * HEAD_W
    past_len = page_table.shape[1] * cache_k.shape[2]
    pool_width = pool_scale.shape[1]

    cos_p, sin_p = _rope_tables(jnp.arange(seq))
    cos_s, sin_s = _rope_tables(past_len + jnp.arange(dec_seq))
    cos_s = jnp.tile(cos_s, (dec_batch, 1))
    sin_s = jnp.tile(sin_s, (dec_batch, 1))

    xp = x_prompt.reshape(batch * seq, d_model)
    xs = x_sample.reshape(dec_batch * dec_seq, d_model)
    outs = {k: [] for k in ("kp", "vp", "pp", "ks", "vs", "ps")}

    for layer in range(depth):
        lam_init = _lambda_init(layer)
        w_in_l = w_in[layer].astype(BF16)
        w_o = w_out[layer, :attn_w].astype(BF16)
        w_p = w_out[layer, attn_w:].astype(BF16)
        pool_w_l = pool_w[layer].astype(BF16)
        g_pre = norm_pre[layer][None, :]
        g_post = norm_post[layer][None, :]
        g_sub = subln_g[layer][None, :]
        sc_l = pool_scale[layer][None, :]
        lamv = jnp.stack([lambda_q1[layer], lambda_k1[layer], lambda_q2[layer], lambda_k2[layer]])

        zp = _inproj(xp, g_pre, w_in_l, cos_p, sin_p, tm=1024, tn=1024)
        op = _attn_prompt(zp, lamv, g_sub, batch=batch, seq=seq, n_heads=n_heads,
                          lam_init=lam_init, tq=512)
        zp3 = zp.reshape(batch, seq, zp.shape[1])
        pp = _pool(zp3, zp3, pool_w_l, sc_l, nb=1, tt=512, start=0, prev_from_z=True, out_dtype=BF16)
        xp = _outproj(op, pp.reshape(batch * seq, pool_width), xp, w_o, w_p, g_post, tm=512)
        outs["kp"].append(zp3[:, :, attn_w:2 * attn_w].reshape(batch, seq, n_heads, HEAD_W))
        outs["vp"].append(zp3[:, :, 2 * attn_w:3 * attn_w].reshape(batch, seq, n_heads, HEAD_W))
        outs["pp"].append(zp3[:, seq - POOL_STATE:, 4 * attn_w:4 * attn_w + pool_width])

        zs = _inproj(xs, g_pre, w_in_l, cos_s, sin_s, tm=dec_batch * dec_seq, tn=1024)
        os_ = _attn_decode(zs, cache_k, cache_v, page_table, lamv, g_sub, layer=layer,
                           t_new=dec_seq, n_heads=n_heads, lam_init=lam_init)
        zs3 = zs.reshape(dec_batch, dec_seq, zs.shape[1])
        state = state_pool[layer]
        prev = jnp.concatenate([jnp.zeros((dec_batch, 1, pool_width), F32), state], axis=1)
        ps = _pool(zs3, prev, pool_w_l, sc_l, nb=16, tt=dec_seq, start=past_len,
                   prev_from_z=False, out_dtype=F32)
        xs = _outproj(os_, ps.reshape(dec_batch * dec_seq, pool_width), xs, w_o, w_p, g_post, tm=512)
        u_s = zs3[:, :, 4 * attn_w:4 * attn_w + pool_width]
        outs["ks"].append(zs3[:, :, attn_w:2 * attn_w].reshape(dec_batch, dec_seq, n_heads, HEAD_W))
        outs["vs"].append(zs3[:, :, 2 * attn_w:3 * attn_w].reshape(dec_batch, dec_seq, n_heads, HEAD_W))
        outs["ps"].append(jnp.concatenate([state, u_s], axis=1)[:, -POOL_STATE:])

    return (xp.reshape(batch, seq, d_model), xs.reshape(dec_batch, dec_seq, d_model),
            jnp.stack(outs["kp"]), jnp.stack(outs["vp"]), jnp.stack(outs["pp"]),
            jnp.stack(outs["ks"]), jnp.stack(outs["vs"]), jnp.stack(outs["ps"]))
```

```python
import functools
import math

import jax
import jax.numpy as jnp
from jax import lax
from jax.experimental import pallas as pl
from jax.experimental.pallas import tpu as pltpu

F32 = jnp.float32
BF16 = jnp.bfloat16

HEAD_DIM = 64
HEAD_W = 2 * HEAD_DIM
POOL_WINDOWS = (2, 4, 8, 16)
POOL_STATE = max(POOL_WINDOWS) - 1
POOL_PREV = POOL_STATE + 1
ROPE_THETA = 10000.0
EPS = 1e-6
NEG = -1e30
ATTN_SCALE = HEAD_DIM ** -0.5

LANES = 128
SUBLANES = 8
VMEM_LIMIT = 56 * 1024 * 1024


def _params(n_axes):
    return pltpu.CompilerParams(
        dimension_semantics=("arbitrary",) * n_axes, vmem_limit_bytes=VMEM_LIMIT)


def _lambda_init(layer):
    return 0.8 - 0.6 * math.exp(-0.3 * layer)


def _lam_from(lamv, lam_init):
    t1 = jnp.sum(lamv[0:1, :] * lamv[1:2, :], axis=-1, keepdims=True)
    t2 = jnp.sum(lamv[2:3, :] * lamv[3:4, :], axis=-1, keepdims=True)
    return jnp.exp(t1) - jnp.exp(t2) + lam_init


def _silu(x):
    return x * (1.0 / (1.0 + jnp.exp(-x)))


def _head_norm_gate(o, g, lam_init):
    ms = jnp.mean(o * o, axis=-1, keepdims=True)
    return (o * lax.rsqrt(ms + EPS) * g) * (1.0 - lam_init)


def _inproj_kernel(x_ref, g_ref, w_ref, cos_ref, sin_ref, z_ref, h_sc, *, n_rope_sections):
    j = pl.program_id(1)

    @pl.when(j == 0)
    def _():
        x = x_ref[...]
        ms = jnp.mean(x * x, axis=-1, keepdims=True)
        h_sc[...] = ((x * lax.rsqrt(ms + EPS)) * g_ref[...]).astype(BF16)

    acc = jnp.dot(h_sc[...], w_ref[...], preferred_element_type=F32)

    @pl.when(j >= n_rope_sections)
    def _():
        z_ref[...] = acc

    @pl.when(j < n_rope_sections)
    def _():
        cos = cos_ref[...]
        sin = sin_ref[...]
        lane = lax.broadcasted_iota(jnp.int32, cos.shape, 1)
        first_half = (lane % HEAD_DIM) < (HEAD_DIM // 2)
        scale = jnp.where(j == 0, ATTN_SCALE, 1.0).astype(F32)
        for hh in range(acc.shape[1] // LANES):
            blk = acc[:, hh * LANES:(hh + 1) * LANES]
            partner = jnp.where(first_half,
                                pltpu.roll(blk, LANES - HEAD_DIM // 2, 1),
                                pltpu.roll(blk, HEAD_DIM // 2, 1))
            z_ref[:, hh * LANES:(hh + 1) * LANES] = (blk * cos + partner * sin) * scale


def _inproj(x2d, g, w_bf16, cos_t, sin_t, *, tm, tn):
    T, D = x2d.shape
    n_cols = w_bf16.shape[1]
    period_blocks = cos_t.shape[0] // tm
    attn_w = n_cols // 6
    kern = functools.partial(_inproj_kernel, n_rope_sections=2 * attn_w // tn)
    return pl.pallas_call(
        kern,
        grid=(T // tm, n_cols // tn),
        in_specs=[
            pl.BlockSpec((tm, D), lambda i, j: (i, 0)),
            pl.BlockSpec((1, D), lambda i, j: (0, 0)),
            pl.BlockSpec((D, tn), lambda i, j: (0, j)),
            pl.BlockSpec((tm, LANES), lambda i, j: (i % period_blocks, 0)),
            pl.BlockSpec((tm, LANES), lambda i, j: (i % period_blocks, 0)),
        ],
        out_specs=pl.BlockSpec((tm, tn), lambda i, j: (i, j)),
        out_shape=jax.ShapeDtypeStruct((T, n_cols), F32),
        scratch_shapes=[pltpu.VMEM((tm, D), BF16)],
        compiler_params=_params(2),
        name="inproj",
    )(x2d, g, w_bf16, cos_t, sin_t)


def _attn_prompt_kernel(q_ref, k_ref, v_ref, ga_ref, lamv_ref, g_ref, o_ref, p_sc, acc_sc, *, lam_init):
    qi = pl.program_id(2)
    tq = q_ref.shape[0]
    q = q_ref[...]
    lane = lax.broadcasted_iota(jnp.int32, q.shape, 1)
    q_maps = [jnp.where((lane >= c * HEAD_DIM) & (lane < (c + 1) * HEAD_DIM), q, 0.0).astype(BF16)
              for c in range(2)]
    nt = (((1,), (1,)), ((), ()))

    def keys_of(tile):
        return k_ref[pl.ds(pl.multiple_of(tile * tq, tq), tq), :].astype(BF16)

    def values_t_of(tile):
        return v_ref[pl.ds(pl.multiple_of(tile * tq, tq), tq), :].T.astype(BF16)

    def scores(k):
        return [lax.dot_general(k, q_maps[c], nt, preferred_element_type=F32) for c in range(2)]

    def softmax_update(c, s_t, m_prev, l_prev, on_diag):
        if on_diag:
            key = lax.broadcasted_iota(jnp.int32, s_t.shape, 0)
            qry = lax.broadcasted_iota(jnp.int32, s_t.shape, 1)
            s_t = jnp.where(key <= qry, s_t, NEG)
        m_new = jnp.maximum(m_prev, jnp.max(s_t, axis=0, keepdims=True))
        alpha = jnp.exp(m_prev - m_new)
        p_t = jnp.exp(s_t - m_new)
        l_new = alpha * l_prev + jnp.sum(p_t, axis=0, keepdims=True)
        p_sc[c] = p_t.astype(BF16)
        return m_new, l_new, alpha

    def apply_values(tile, alphas):
        v_t = values_t_of(tile)
        pv = [jnp.dot(v_t, p_sc[c], preferred_element_type=F32) for c in range(2)]
        for c in range(2):
            acc_sc[c] = alphas[c] * acc_sc[c] + pv[c]

    acc_sc[...] = jnp.zeros(acc_sc.shape, F32)
    s = scores(keys_of(qi))
    stats = tuple(softmax_update(c, s[c], jnp.full((1, tq), NEG, F32), jnp.zeros((1, tq), F32), True)
                  for c in range(2))

    def step(j, stats):
        s = scores(keys_of(j))
        apply_values(jnp.where(j == 0, qi, j - 1), [stats[c][2] for c in range(2)])
        return tuple(softmax_update(c, s[c], stats[c][0], stats[c][1], False) for c in range(2))

    stats = lax.fori_loop(0, qi, step, stats)
    apply_values(jnp.where(qi == 0, qi, qi - 1), [stats[c][2] for c in range(2)])
    l0, l1 = stats[0][1], stats[1][1]

    lam = _lam_from(lamv_ref[...], lam_init)
    o = (acc_sc[0] / l0 - lam * (acc_sc[1] / l1)).T
    o = _head_norm_gate(o, g_ref[...], lam_init)
    o_ref[...] = (o * _silu(ga_ref[...])).astype(o_ref.dtype)


def _attn_prompt(z, lamv, subln_g, *, batch, seq, n_heads, lam_init, tq):
    nq = seq // tq
    attn_w = n_heads * HEAD_W
    kb = attn_w // LANES
    kern = functools.partial(_attn_prompt_kernel, lam_init=lam_init)
    return pl.pallas_call(
        kern,
        grid=(batch, n_heads, nq),
        in_specs=[
            pl.BlockSpec((tq, LANES), lambda b, h, i: (b * nq + i, h)),
            pl.BlockSpec((seq, LANES), lambda b, h, i: (b, kb + h)),
            pl.BlockSpec((seq, LANES), lambda b, h, i: (b, 2 * kb + h)),
            pl.BlockSpec((tq, LANES), lambda b, h, i: (b * nq + i, 3 * kb + h)),
            pl.BlockSpec((4, HEAD_DIM), lambda b, h, i: (0, 0)),
            pl.BlockSpec((1, HEAD_W), lambda b, h, i: (0, 0)),
        ],
        out_specs=pl.BlockSpec((tq, LANES), lambda b, h, i: (b * nq + i, h)),
        out_shape=jax.ShapeDtypeStruct((batch * seq, attn_w), BF16),
        scratch_shapes=[pltpu.VMEM((2, tq, tq), BF16), pltpu.VMEM((2, HEAD_W, tq), F32)],
        compiler_params=_params(3),
        name="attn_prompt",
    )(z, z, z, z, lamv, subln_g)


def _attn_decode_kernel(pt_ref, q_ref, kn_ref, vn_ref, ga_ref, lamv_ref, g_ref, *rest,
                        lam_init, n_pages, n_heads):
    del pt_ref
    k_pages = rest[:n_pages]
    v_pages = rest[n_pages:2 * n_pages]
    o_ref = rest[2 * n_pages]
    t_new = q_ref.shape[0]
    n_rows = 2 * n_heads * t_new

    q = q_ref[...]
    q_rep = jnp.concatenate([q] * (2 * n_heads), axis=0)
    row = lax.broadcasted_iota(jnp.int32, q_rep.shape, 0)
    col = lax.broadcasted_iota(jnp.int32, q_rep.shape, 1)
    q_bd = jnp.where((col // HEAD_DIM) == (row // t_new), q_rep, 0.0).astype(BF16)

    page = k_pages[0].shape[0] // n_heads

    def gather_heads(page_ref):
        return jnp.concatenate(
            [page_ref[pl.ds(h, page, stride=n_heads), :] for h in range(n_heads)], axis=1).astype(BF16)

    nt = (((1,), (1,)), ((), ()))
    s_past = jnp.concatenate(
        [lax.dot_general(q_bd, gather_heads(k_pages[j]), nt, preferred_element_type=F32)
         for j in range(n_pages)], axis=1)
    s_new = lax.dot_general(q_bd, kn_ref[...].astype(BF16), nt, preferred_element_type=F32)
    r_new = lax.broadcasted_iota(jnp.int32, s_new.shape, 0)
    c_new = lax.broadcasted_iota(jnp.int32, s_new.shape, 1)
    s_new = jnp.where(c_new <= (r_new % t_new), s_new, NEG)

    m = jnp.maximum(jnp.max(s_past, axis=-1, keepdims=True),
                    jnp.max(s_new, axis=-1, keepdims=True))
    p_past = jnp.exp(s_past - m)
    p_new = jnp.exp(s_new - m)
    inv_l = 1.0 / (jnp.sum(p_past, axis=-1, keepdims=True) + jnp.sum(p_new, axis=-1, keepdims=True))
    p_past = (p_past * inv_l).astype(BF16)
    p_new = (p_new * inv_l).astype(BF16)

    acc = jnp.dot(p_new, vn_ref[...].astype(BF16), preferred_element_type=F32)
    for j in range(n_pages):
        acc = acc + jnp.dot(p_past[:, j * page:(j + 1) * page], gather_heads(v_pages[j]),
                            preferred_element_type=F32)

    lam = _lam_from(lamv_ref[...], lam_init)
    g = g_ref[...]
    outs = []
    for h in range(n_heads):
        r0 = h * 2 * t_new
        o0 = acc[r0:r0 + t_new, h * HEAD_W:(h + 1) * HEAD_W]
        o1 = acc[r0 + t_new:r0 + 2 * t_new, h * HEAD_W:(h + 1) * HEAD_W]
        outs.append(_head_norm_gate(o0 - lam * o1, g, lam_init))
    o_ref[...] = jnp.concatenate(outs, axis=1) * _silu(ga_ref[...])


def _attn_decode(z, cache_k, cache_v, page_table, lamv, subln_g, *, layer, t_new, n_heads, lam_init):
    dec_batch, n_pages = page_table.shape
    depth, n_phys, page = cache_k.shape[:3]
    attn_w = n_heads * HEAD_W
    cache_k = cache_k.reshape(depth, n_phys, page * n_heads, HEAD_W)
    cache_v = cache_v.reshape(depth, n_phys, page * n_heads, HEAD_W)
    row_spec = lambda sec: pl.BlockSpec((t_new, attn_w), lambda b, pt: (b, sec))
    page_specs = [
        pl.BlockSpec((None, None, page * n_heads, HEAD_W),
                     lambda b, pt, j=j: (layer, pt[b, j], 0, 0))
        for j in range(n_pages)
    ]
    kern = functools.partial(_attn_decode_kernel, lam_init=lam_init, n_pages=n_pages, n_heads=n_heads)
    grid_spec = pltpu.PrefetchScalarGridSpec(
        num_scalar_prefetch=1,
        grid=(dec_batch,),
        in_specs=[row_spec(0), row_spec(1), row_spec(2), row_spec(3),
                  pl.BlockSpec((4, HEAD_DIM), lambda b, pt: (0, 0)),
                  pl.BlockSpec((1, HEAD_W), lambda b, pt: (0, 0))] + page_specs + page_specs,
        out_specs=pl.BlockSpec((t_new, attn_w), lambda b, pt: (b, 0)),
    )
    return pl.pallas_call(
        kern,
        grid_spec=grid_spec,
        out_shape=jax.ShapeDtypeStruct((dec_batch * t_new, attn_w), F32),
        compiler_params=_params(1),
        name="attn_decode",
    )(page_table, z, z, z, z, lamv, subln_g, *([cache_k] * n_pages), *([cache_v] * n_pages))


def _pool_kernel(u_ref, prev_ref, gp_ref, w_ref, sc_ref, p_ref, ext_sc, *, start, zero_first_prefix):
    i = pl.program_id(1)
    nb, tt, width = u_ref.shape
    gw = width // len(POOL_WINDOWS)
    prev = prev_ref[...]
    if zero_first_prefix:
        prev = jnp.where(i == 0, 0.0, prev)
    u = u_ref[...]
    ext_sc[:, 0:POOL_PREV, :] = prev
    ext_sc[:, POOL_PREV:POOL_PREV + tt, :] = u
    pos = start + i * tt + lax.broadcasted_iota(jnp.int32, (1, tt, 1), 1)
    for g, w in enumerate(POOL_WINDOWS):
        cols = slice(g * gw, (g + 1) * gw)
        ssum = u[:, :, cols]
        for back in range(1, w):
            ssum = ssum + ext_sc[:, POOL_PREV - back:POOL_PREV - back + tt, cols]
        cnt = jnp.minimum(pos + 1, w).astype(F32)
        d = (ssum / cnt - u[:, :, cols]).astype(BF16).reshape(nb * tt, gw)
        y = jnp.dot(d, w_ref[g], preferred_element_type=F32) * sc_ref[:, cols]
        y = y * _silu(gp_ref[:, :, cols].reshape(nb * tt, gw))
        p_ref[:, :, cols] = y.reshape(nb, tt, gw).astype(p_ref.dtype)


def _pool(z3, prev_src, pool_w_bf16, pool_scale, *, nb, tt, start, prev_from_z, out_dtype):
    n_seq, seq, n_cols = z3.shape
    width = n_cols // 6
    u_blk, gp_blk = 4, 5
    if prev_from_z:
        per = tt // POOL_PREV
        prev_spec = pl.BlockSpec((nb, POOL_PREV, width),
                                 lambda b, i: (b, jnp.maximum(i * per - 1, 0), u_blk))
    else:
        prev_spec = pl.BlockSpec((nb, POOL_PREV, width), lambda b, i: (b, 0, 0))
    kern = functools.partial(_pool_kernel, start=start, zero_first_prefix=prev_from_z)
    return pl.pallas_call(
        kern,
        grid=(n_seq // nb, seq // tt),
        in_specs=[
            pl.BlockSpec((nb, tt, width), lambda b, i: (b, i, u_blk)),
            prev_spec,
            pl.BlockSpec((nb, tt, width), lambda b, i: (b, i, gp_blk)),
            pl.BlockSpec(pool_w_bf16.shape, lambda b, i: (0, 0, 0)),
            pl.BlockSpec((1, width), lambda b, i: (0, 0)),
        ],
        out_specs=pl.BlockSpec((nb, tt, width), lambda b, i: (b, i, 0)),
        out_shape=jax.ShapeDtypeStruct((n_seq, seq, width), out_dtype),
        scratch_shapes=[pltpu.VMEM((nb, POOL_PREV + tt, width), F32)],
        compiler_params=_params(2),
        name="pool",
    )(z3, prev_src, z3, pool_w_bf16, pool_scale)


def _outproj_kernel(o_ref, p_ref, x_ref, wo_ref, wp_ref, g_ref, out_ref):
    y = (jnp.dot(o_ref[...].astype(BF16), wo_ref[...], preferred_element_type=F32)
         + jnp.dot(p_ref[...].astype(BF16), wp_ref[...], preferred_element_type=F32))
    ms = jnp.mean(y * y, axis=-1, keepdims=True)
    out_ref[...] = x_ref[...] + (y * lax.rsqrt(ms + EPS)) * g_ref[...]


def _outproj(o, p, x2d, w_o, w_p, g, *, tm):
    T, D = x2d.shape
    half = o.shape[1]
    return pl.pallas_call(
        _outproj_kernel,
        grid=(T // tm,),
        in_specs=[
            pl.BlockSpec((tm, half), lambda i: (i, 0)),
            pl.BlockSpec((tm, half), lambda i: (i, 0)),
            pl.BlockSpec((tm, D), lambda i: (i, 0)),
            pl.BlockSpec((half, D), lambda i: (0, 0)),
            pl.BlockSpec((half, D), lambda i: (0, 0)),
            pl.BlockSpec((1, D), lambda i: (0, 0)),
        ],
        out_specs=pl.BlockSpec((tm, D), lambda i: (i, 0)),
        out_shape=jax.ShapeDtypeStruct((T, D), F32),
        compiler_params=_params(1),
        name="outproj",
    )(o, p, x2d, w_o, w_p, g)


def _rope_tables(pos):
    half = HEAD_DIM // 2
    inv = ROPE_THETA ** (-jnp.arange(half, dtype=F32) * 2.0 / HEAD_DIM)
    ang = pos.astype(F32)[:, None] * inv[None, :]
    reps = LANES // half
    cos = jnp.tile(jnp.cos(ang), (1, reps))
    sin = jnp.tile(jnp.sin(ang), (1, reps))
    lane = jnp.arange(LANES)
    sign = jnp.where((lane % HEAD_DIM) < half, -1.0, 1.0).astype(F32)
    return cos, sin * sign[None, :]


def kernel(x_prompt, x_sample, cache_k, cache_v, state_pool, page_table, norm_pre, norm_post, w_in,
           lambda_q1, lambda_k1, lambda_q2, lambda_k2, subln_g, pool_w, pool_scale, w_out):
    batch, seq, d_model = x_prompt.shape
    dec_batch, dec_seq, _ = x_sample.shape
    depth = w_in.shape[0]
    n_heads = cache_k.shape[3]
    attn_w = n_heads * HEAD_W
    past_len = page_table.shape[1] * cache_k.shape[2]
    pool_width = pool_scale.shape[1]

    cos_p, sin_p = _rope_tables(jnp.arange(seq))
    cos_s, sin_s = _rope_tables(past_len + jnp.arange(dec_seq))
    cos_s = jnp.tile(cos_s, (dec_batch, 1))
    sin_s = jnp.tile(sin_s, (dec_batch, 1))

    xp = x_prompt.reshape(batch * seq, d_model)
    xs = x_sample.reshape(dec_batch * dec_seq, d_model)
    outs = {k: [] for k in ("kp", "vp", "pp", "ks", "vs", "ps")}

    for layer in range(depth):
        lam_init = _lambda_init(layer)
        w_in_l = w_in[layer].astype(BF16)
        w_o = w_out[layer, :attn_w].astype(BF16)
        w_p = w_out[layer, attn_w:].astype(BF16)
        pool_w_l = pool_w[layer].astype(BF16)
        g_pre = norm_pre[layer][None, :]
        g_post = norm_post[layer][None, :]
        g_sub = subln_g[layer][None, :]
        sc_l = pool_scale[layer][None, :]
        lamv = jnp.stack([lambda_q1[layer], lambda_k1[layer], lambda_q2[layer], lambda_k2[layer]])

        zp = _inproj(xp, g_pre, w_in_l, cos_p, sin_p, tm=1024, tn=1024)
        op = _attn_prompt(zp, lamv, g_sub, batch=batch, seq=seq, n_heads=n_heads,
                          lam_init=lam_init, tq=512)
        zp3 = zp.reshape(batch, seq, zp.shape[1])
        pp = _pool(zp3, zp3, pool_w_l, sc_l, nb=1, tt=512, start=0, prev_from_z=True, out_dtype=BF16)
        xp = _outproj(op, pp.reshape(batch * seq, pool_width), xp, w_o, w_p, g_post, tm=512)
        outs["kp"].append(zp3[:, :, attn_w:2 * attn_w].reshape(batch, seq, n_heads, HEAD_W))
        outs["vp"].append(zp3[:, :, 2 * attn_w:3 * attn_w].reshape(batch, seq, n_heads, HEAD_W))
        outs["pp"].append(zp3[:, seq - POOL_STATE:, 4 * attn_w:4 * attn_w + pool_width])

        zs = _inproj(xs, g_pre, w_in_l, cos_s, sin_s, tm=dec_batch * dec_seq, tn=1024)
        os_ = _attn_decode(zs, cache_k, cache_v, page_table, lamv, g_sub, layer=layer,
                           t_new=dec_seq, n_heads=n_heads, lam_init=lam_init)
        zs3 = zs.reshape(dec_batch, dec_seq, zs.shape[1])
        state = state_pool[layer]
        prev = jnp.concatenate([jnp.zeros((dec_batch, 1, pool_width), F32), state], axis=1)
        ps = _pool(zs3, prev, pool_w_l, sc_l, nb=16, tt=dec_seq, start=past_len,
                   prev_from_z=False, out_dtype=F32)
        xs = _outproj(os_, ps.reshape(dec_batch * dec_seq, pool_width), xs, w_o, w_p, g_post, tm=512)
        u_s = zs3[:, :, 4 * attn_w:4 * attn_w + pool_width]
        outs["ks"].append(zs3[:, :, attn_w:2 * attn_w].reshape(dec_batch, dec_seq, n_heads, HEAD_W))
        outs["vs"].append(zs3[:, :, 2 * attn_w:3 * attn_w].reshape(dec_batch, dec_seq, n_heads, HEAD_W))
        outs["ps"].append(jnp.concatenate([state, u_s], axis=1)[:, -POOL_STATE:])

    return (xp.reshape(batch, seq, d_model), xs.reshape(dec_batch, dec_seq, d_model),
            jnp.stack(outs["kp"]), jnp.stack(outs["vp"]), jnp.stack(outs["pp"]),
            jnp.stack(outs["ks"]), jnp.stack(outs["vs"]), jnp.stack(outs["ps"]))
```

```python
import functools
import math

import jax
import jax.numpy as jnp
from jax import lax
from jax.experimental import pallas as pl
from jax.experimental.pallas import tpu as pltpu

F32 = jnp.float32
BF16 = jnp.bfloat16

HEAD_DIM = 64
HEAD_W = 2 * HEAD_DIM
POOL_WINDOWS = (2, 4, 8, 16)
POOL_STATE = max(POOL_WINDOWS) - 1
POOL_PREV = POOL_STATE + 1
ROPE_THETA = 10000.0
EPS = 1e-6
NEG = -1e30
ATTN_SCALE = HEAD_DIM ** -0.5

LANES = 128
SUBLANES = 8
VMEM_LIMIT = 56 * 1024 * 1024


def _params(n_axes):
    return pltpu.CompilerParams(
        dimension_semantics=("arbitrary",) * n_axes, vmem_limit_bytes=VMEM_LIMIT)


def _lambda_init(layer):
    return 0.8 - 0.6 * math.exp(-0.3 * layer)


def _lam_from(lamv, lam_init):
    t1 = jnp.sum(lamv[0:1, :] * lamv[1:2, :], axis=-1, keepdims=True)
    t2 = jnp.sum(lamv[2:3, :] * lamv[3:4, :], axis=-1, keepdims=True)
    return jnp.exp(t1) - jnp.exp(t2) + lam_init


def _silu(x):
    return x * (1.0 / (1.0 + jnp.exp(-x)))


def _head_norm_gate(o, g, lam_init):
    ms = jnp.mean(o * o, axis=-1, keepdims=True)
    return (o * lax.rsqrt(ms + EPS) * g) * (1.0 - lam_init)


def _inproj_kernel(x_ref, g_ref, w_ref, cos_ref, sin_ref, z_ref, h_sc, *, n_rope_sections):
    j = pl.program_id(1)

    @pl.when(j == 0)
    def _():
        x = x_ref[...]
        ms = jnp.mean(x * x, axis=-1, keepdims=True)
        h_sc[...] = ((x * lax.rsqrt(ms + EPS)) * g_ref[...]).astype(BF16)

    acc = jnp.dot(h_sc[...], w_ref[...], preferred_element_type=F32)

    @pl.when(j >= n_rope_sections)
    def _():
        z_ref[...] = acc

    @pl.when(j < n_rope_sections)
    def _():
        cos = cos_ref[...]
        sin = sin_ref[...]
        lane = lax.broadcasted_iota(jnp.int32, cos.shape, 1)
        first_half = (lane % HEAD_DIM) < (HEAD_DIM // 2)
        scale = jnp.where(j == 0, ATTN_SCALE, 1.0).astype(F32)
        for hh in range(acc.shape[1] // LANES):
            blk = acc[:, hh * LANES:(hh + 1) * LANES]
            partner = jnp.where(first_half,
                                pltpu.roll(blk, LANES - HEAD_DIM // 2, 1),
                                pltpu.roll(blk, HEAD_DIM // 2, 1))
            z_ref[:, hh * LANES:(hh + 1) * LANES] = (blk * cos + partner * sin) * scale


def _inproj(x2d, g, w_bf16, cos_t, sin_t, *, tm, tn):
    T, D = x2d.shape
    n_cols = w_bf16.shape[1]
    period_blocks = cos_t.shape[0] // tm
    attn_w = n_cols // 6
    kern = functools.partial(_inproj_kernel, n_rope_sections=2 * attn_w // tn)
    return pl.pallas_call(
        kern,
        grid=(T // tm, n_cols // tn),
        in_specs=[
            pl.BlockSpec((tm, D), lambda i, j: (i, 0)),
            pl.BlockSpec((1, D), lambda i, j: (0, 0)),
            pl.BlockSpec((D, tn), lambda i, j: (0, j)),
            pl.BlockSpec((tm, LANES), lambda i, j: (i % period_blocks, 0)),
            pl.BlockSpec((tm, LANES), lambda i, j: (i % period_blocks, 0)),
        ],
        out_specs=pl.BlockSpec((tm, tn), lambda i, j: (i, j)),
        out_shape=jax.ShapeDtypeStruct((T, n_cols), F32),
        scratch_shapes=[pltpu.VMEM((tm, D), BF16)],
        compiler_params=_params(2),
        name="inproj",
    )(x2d, g, w_bf16, cos_t, sin_t)


def _attn_prompt_kernel(q_ref, k_ref, v_ref, ga_ref, lamv_ref, g_ref, o_ref, p_sc, acc_sc, *, lam_init):
    qi = pl.program_id(2)
    tq = q_ref.shape[0]
    q = q_ref[...]
    lane = lax.broadcasted_iota(jnp.int32, q.shape, 1)
    q_maps = [jnp.where((lane >= c * HEAD_DIM) & (lane < (c + 1) * HEAD_DIM), q, 0.0).astype(BF16)
              for c in range(2)]
    nt = (((1,), (1,)), ((), ()))

    def keys_of(tile):
        return k_ref[pl.ds(pl.multiple_of(tile * tq, tq), tq), :].astype(BF16)

    def values_t_of(tile):
        return v_ref[pl.ds(pl.multiple_of(tile * tq, tq), tq), :].T.astype(BF16)

    def scores(k):
        return [lax.dot_general(k, q_maps[c], nt, preferred_element_type=F32) for c in range(2)]

    def softmax_update(c, s_t, m_prev, l_prev, on_diag):
        if on_diag:
            key = lax.broadcasted_iota(jnp.int32, s_t.shape, 0)
            qry = lax.broadcasted_iota(jnp.int32, s_t.shape, 1)
            s_t = jnp.where(key <= qry, s_t, NEG)
        m_new = jnp.maximum(m_prev, jnp.max(s_t, axis=0, keepdims=True))
        alpha = jnp.exp(m_prev - m_new)
        p_t = jnp.exp(s_t - m_new)
        l_new = alpha * l_prev + jnp.sum(p_t, axis=0, keepdims=True)
        p_sc[c] = p_t.astype(BF16)
        return m_new, l_new, alpha

    def apply_values(tile, alphas):
        v_t = values_t_of(tile)
        pv = [jnp.dot(v_t, p_sc[c], preferred_element_type=F32) for c in range(2)]
        for c in range(2):
            acc_sc[c] = alphas[c] * acc_sc[c] + pv[c]

    acc_sc[...] = jnp.zeros(acc_sc.shape, F32)
    s = scores(keys_of(qi))
    stats = tuple(softmax_update(c, s[c], jnp.full((1, tq), NEG, F32), jnp.zeros((1, tq), F32), True)
                  for c in range(2))

    def step(j, stats):
        s = scores(keys_of(j))
        apply_values(jnp.where(j == 0, qi, j - 1), [stats[c][2] for c in range(2)])
        return tuple(softmax_update(c, s[c], stats[c][0], stats[c][1], False) for c in range(2))

    stats = lax.fori_loop(0, qi, step, stats)
    apply_values(jnp.where(qi == 0, qi, qi - 1), [stats[c][2] for c in range(2)])
    l0, l1 = stats[0][1], stats[1][1]

    lam = _lam_from(lamv_ref[...], lam_init)
    o = (acc_sc[0] / l0 - lam * (acc_sc[1] / l1)).T
    o = _head_norm_gate(o, g_ref[...], lam_init)
    o_ref[...] = (o * _silu(ga_ref[...])).astype(o_ref.dtype)


def _attn_prompt(z, lamv, subln_g, *, batch, seq, n_heads, lam_init, tq):
    nq = seq // tq
    attn_w = n_heads * HEAD_W
    kb = attn_w // LANES
    kern = functools.partial(_attn_prompt_kernel, lam_init=lam_init)
    return pl.pallas_call(
        kern,
        grid=(batch, n_heads, nq),
        in_specs=[
            pl.BlockSpec((tq, LANES), lambda b, h, i: (b * nq + i, h)),
            pl.BlockSpec((seq, LANES), lambda b, h, i: (b, kb + h)),
            pl.BlockSpec((seq, LANES), lambda b, h, i: (b, 2 * kb + h)),
            pl.BlockSpec((tq, LANES), lambda b, h, i: (b * nq + i, 3 * kb + h)),
            pl.BlockSpec((4, HEAD_DIM), lambda b, h, i: (0, 0)),
            pl.BlockSpec((1, HEAD_W), lambda b, h, i: (0, 0)),
        ],
        out_specs=pl.BlockSpec((tq, LANES), lambda b, h, i: (b * nq + i, h)),
        out_shape=jax.ShapeDtypeStruct((batch * seq, attn_w), BF16),
        scratch_shapes=[pltpu.VMEM((2, tq, tq), BF16), pltpu.VMEM((2, HEAD_W, tq), F32)],
        compiler_params=_params(3),
        name="attn_prompt",
    )(z, z, z, z, lamv, subln_g)


def _attn_decode_kernel(pt_ref, q_ref, kn_ref, vn_ref, ga_ref, lamv_ref, g_ref, *rest,
                        lam_init, n_pages, n_heads):
    del pt_ref
    k_pages = rest[:n_pages]
    v_pages = rest[n_pages:2 * n_pages]
    o_ref = rest[2 * n_pages]
    t_new = q_ref.shape[0]
    page = k_pages[0].shape[0] // n_heads
    half = n_heads // 2
    nt = (((1,), (1,)), ((), ()))

    def pair_rows(pages, h):
        return jnp.concatenate(
            [ref[pl.ds(h, 2 * page, stride=half), :] for ref in pages], axis=0).astype(BF16)

    lam = _lam_from(lamv_ref[...], lam_init)
    g = g_ref[...]
    n_rows = 4 * t_new
    row = lax.broadcasted_iota(jnp.int32, (n_rows, HEAD_W), 0)
    lane = lax.broadcasted_iota(jnp.int32, (n_rows, HEAD_W), 1)
    in_map = (lane // HEAD_DIM) == ((row // t_new) % 2)
    past = 2 * page * n_pages
    row_e = lax.broadcasted_iota(jnp.int32, (n_rows, past), 0) // (2 * t_new)
    own_past = (lax.broadcasted_iota(jnp.int32, (n_rows, past), 1) % 2) == row_e
    r_new = lax.broadcasted_iota(jnp.int32, (n_rows, 2 * t_new), 0)
    c_new = lax.broadcasted_iota(jnp.int32, (n_rows, 2 * t_new), 1)
    own_new = ((c_new // t_new) == (r_new // (2 * t_new))) & ((c_new % t_new) <= (r_new % t_new))

    outs = [None] * n_heads
    for h in range(half):
        cols = [slice((h + e * half) * HEAD_W, (h + e * half + 1) * HEAD_W) for e in range(2)]
        q_rep = jnp.concatenate([q_ref[:, cols[0]]] * 2 + [q_ref[:, cols[1]]] * 2, axis=0)
        q4 = jnp.where(in_map, q_rep, 0.0).astype(BF16)
        k_new = jnp.concatenate([kn_ref[:, cols[0]], kn_ref[:, cols[1]]], axis=0).astype(BF16)
        v_new = jnp.concatenate([vn_ref[:, cols[0]], vn_ref[:, cols[1]]], axis=0).astype(BF16)
        s_past = lax.dot_general(q4, pair_rows(k_pages, h), nt, preferred_element_type=F32)
        s_past = jnp.where(own_past, s_past, NEG)
        s_new = lax.dot_general(q4, k_new, nt, preferred_element_type=F32)
        s_new = jnp.where(own_new, s_new, NEG)
        m = jnp.maximum(jnp.max(s_past, axis=-1, keepdims=True), jnp.max(s_new, axis=-1, keepdims=True))
        p_past = jnp.exp(s_past - m)
        p_new = jnp.exp(s_new - m)
        l = jnp.sum(p_past, axis=-1, keepdims=True) + jnp.sum(p_new, axis=-1, keepdims=True)
        o4 = (jnp.dot(p_past.astype(BF16), pair_rows(v_pages, h), preferred_element_type=F32)
              + jnp.dot(p_new.astype(BF16), v_new, preferred_element_type=F32)) / l
        for e in range(2):
            r0 = e * 2 * t_new
            outs[h + e * half] = _head_norm_gate(
                o4[r0:r0 + t_new] - lam * o4[r0 + t_new:r0 + 2 * t_new], g, lam_init)
    o_ref[...] = jnp.concatenate(outs, axis=1) * _silu(ga_ref[...])


def _attn_decode(z, cache_k, cache_v, page_table, lamv, subln_g, *, layer, t_new, n_heads, lam_init):
    dec_batch, n_pages = page_table.shape
    depth, n_phys, page = cache_k.shape[:3]
    attn_w = n_heads * HEAD_W
    cache_k = cache_k.reshape(depth, n_phys, page * n_heads, HEAD_W)
    cache_v = cache_v.reshape(depth, n_phys, page * n_heads, HEAD_W)
    row_spec = lambda sec: pl.BlockSpec((t_new, attn_w), lambda b, pt: (b, sec))
    page_specs = [
        pl.BlockSpec((None, None, page * n_heads, HEAD_W),
                     lambda b, pt, j=j: (layer, pt[b, j], 0, 0))
        for j in range(n_pages)
    ]
    kern = functools.partial(_attn_decode_kernel, lam_init=lam_init, n_pages=n_pages, n_heads=n_heads)
    grid_spec = pltpu.PrefetchScalarGridSpec(
        num_scalar_prefetch=1,
        grid=(dec_batch,),
        in_specs=[row_spec(0), row_spec(1), row_spec(2), row_spec(3),
                  pl.BlockSpec((4, HEAD_DIM), lambda b, pt: (0, 0)),
                  pl.BlockSpec((1, HEAD_W), lambda b, pt: (0, 0))] + page_specs + page_specs,
        out_specs=pl.BlockSpec((t_new, attn_w), lambda b, pt: (b, 0)),
    )
    return pl.pallas_call(
        kern,
        grid_spec=grid_spec,
        out_shape=jax.ShapeDtypeStruct((dec_batch * t_new, attn_w), F32),
        compiler_params=_params(1),
        name="attn_decode",
    )(page_table, z, z, z, z, lamv, subln_g, *([cache_k] * n_pages), *([cache_v] * n_pages))


def _pool_kernel(u_ref, prev_ref, gp_ref, w_ref, sc_ref, p_ref, ext_sc, *, start, zero_first_prefix):
    i = pl.program_id(1)
    nb, tt, width = u_ref.shape
    gw = width // len(POOL_WINDOWS)
    prev = prev_ref[...]
    if zero_first_prefix:
        prev = jnp.where(i == 0, 0.0, prev)
    u = u_ref[...]
    ext_sc[:, 0:POOL_PREV, :] = prev
    ext_sc[:, POOL_PREV:POOL_PREV + tt, :] = u
    pos = start + i * tt + lax.broadcasted_iota(jnp.int32, (1, tt, 1), 1)
    for g, w in enumerate(POOL_WINDOWS):
        cols = slice(g * gw, (g + 1) * gw)
        ssum = u[:, :, cols]
        for back in range(1, w):
            ssum = ssum + ext_sc[:, POOL_PREV - back:POOL_PREV - back + tt, cols]
        cnt = jnp.minimum(pos + 1, w).astype(F32)
        d = (ssum / cnt - u[:, :, cols]).astype(BF16).reshape(nb * tt, gw)
        y = jnp.dot(d, w_ref[g], preferred_element_type=F32) * sc_ref[:, cols]
        y = y * _silu(gp_ref[:, :, cols].reshape(nb * tt, gw))
        p_ref[:, :, cols] = y.reshape(nb, tt, gw).astype(p_ref.dtype)


def _pool(z3, prev_src, pool_w_bf16, pool_scale, *, nb, tt, start, prev_from_z, out_dtype):
    n_seq, seq, n_cols = z3.shape
    width = n_cols // 6
    u_blk, gp_blk = 4, 5
    if prev_from_z:
        per = tt // POOL_PREV
        prev_spec = pl.BlockSpec((nb, POOL_PREV, width),
                                 lambda b, i: (b, jnp.maximum(i * per - 1, 0), u_blk))
    else:
        prev_spec = pl.BlockSpec((nb, POOL_PREV, width), lambda b, i: (b, 0, 0))
    kern = functools.partial(_pool_kernel, start=start, zero_first_prefix=prev_from_z)
    return pl.pallas_call(
        kern,
        grid=(n_seq // nb, seq // tt),
        in_specs=[
            pl.BlockSpec((nb, tt, width), lambda b, i: (b, i, u_blk)),
            prev_spec,
            pl.BlockSpec((nb, tt, width), lambda b, i: (b, i, gp_blk)),
            pl.BlockSpec(pool_w_bf16.shape, lambda b, i: (0, 0, 0)),
            pl.BlockSpec((1, width), lambda b, i: (0, 0)),
        ],
        out_specs=pl.BlockSpec((nb, tt, width), lambda b, i: (b, i, 0)),
        out_shape=jax.ShapeDtypeStruct((n_seq, seq, width), out_dtype),
        scratch_shapes=[pltpu.VMEM((nb, POOL_PREV + tt, width), F32)],
        compiler_params=_params(2),
        name="pool",
    )(z3, prev_src, z3, pool_w_bf16, pool_scale)


def _outproj_kernel(o_ref, p_ref, x_ref, wo_ref, wp_ref, g_ref, out_ref):
    y = (jnp.dot(o_ref[...].astype(BF16), wo_ref[...], preferred_element_type=F32)
         + jnp.dot(p_ref[...].astype(BF16), wp_ref[...], preferred_element_type=F32))
    ms = jnp.mean(y * y, axis=-1, keepdims=True)
    out_ref[...] = x_ref[...] + (y * lax.rsqrt(ms + EPS)) * g_ref[...]


def _outproj(o, p, x2d, w_o, w_p, g, *, tm):
    T, D = x2d.shape
    half = o.shape[1]
    return pl.pallas_call(
        _outproj_kernel,
        grid=(T // tm,),
        in_specs=[
            pl.BlockSpec((tm, half), lambda i: (i, 0)),
            pl.BlockSpec((tm, half), lambda i: (i, 0)),
            pl.BlockSpec((tm, D), lambda i: (i, 0)),
            pl.BlockSpec((half, D), lambda i: (0, 0)),
            pl.BlockSpec((half, D), lambda i: (0, 0)),
            pl.BlockSpec((1, D), lambda i: (0, 0)),
        ],
        out_specs=pl.BlockSpec((tm, D), lambda i: (i, 0)),
        out_shape=jax.ShapeDtypeStruct((T, D), F32),
        compiler_params=_params(1),
        name="outproj",
    )(o, p, x2d, w_o, w_p, g)


def _rope_tables(pos):
    half = HEAD_DIM // 2
    inv = ROPE_THETA ** (-jnp.arange(half, dtype=F32) * 2.0 / HEAD_DIM)
    ang = pos.astype(F32)[:, None] * inv[None, :]
    reps = LANES // half
    cos = jnp.tile(jnp.cos(ang), (1, reps))
    sin = jnp.tile(jnp.sin(ang), (1, reps))
    lane = jnp.arange(LANES)
    sign = jnp.where((lane % HEAD_DIM) < half, -1.0, 1.0).astype(F32)
    return cos, sin * sign[None, :]


def kernel(x_prompt, x_sample, cache_k, cache_v, state_pool, page_table, norm_pre, norm_post, w_in,
           lambda_q1, lambda_k1, lambda_q2, lambda_k2, subln_g, pool_w, pool_scale, w_out):
    batch, seq, d_model = x_prompt.shape
    dec_batch, dec_seq, _ = x_sample.shape
    depth = w_in.shape[0]
    n_heads = cache_k.shape[3]
    attn_w = n_heads * HEAD_W
    past_len = page_table.shape[1] * cache_k.shape[2]
    pool_width = pool_scale.shape[1]

    cos_p, sin_p = _rope_tables(jnp.arange(seq))
    cos_s, sin_s = _rope_tables(past_len + jnp.arange(dec_seq))
    cos_s = jnp.tile(cos_s, (dec_batch, 1))
    sin_s = jnp.tile(sin_s, (dec_batch, 1))

    xp = x_prompt.reshape(batch * seq, d_model)
    xs = x_sample.reshape(dec_batch * dec_seq, d_model)
    outs = {k: [] for k in ("kp", "vp", "pp", "ks", "vs", "ps")}

    for layer in range(depth):
        lam_init = _lambda_init(layer)
        w_in_l = w_in[layer].astype(BF16)
        w_o = w_out[layer, :attn_w].astype(BF16)
        w_p = w_out[layer, attn_w:].astype(BF16)
        pool_w_l = pool_w[layer].astype(BF16)
        g_pre = norm_pre[layer][None, :]
        g_post = norm_post[layer][None, :]
        g_sub = subln_g[layer][None, :]
        sc_l = pool_scale[layer][None, :]
        lamv = jnp.stack([lambda_q1[layer], lambda_k1[layer], lambda_q2[layer], lambda_k2[layer]])

        zp = _inproj(xp, g_pre, w_in_l, cos_p, sin_p, tm=1024, tn=1024)
        op = _attn_prompt(zp, lamv, g_sub, batch=batch, seq=seq, n_heads=n_heads,
                          lam_init=lam_init, tq=512)
        zp3 = zp.reshape(batch, seq, zp.shape[1])
        pp = _pool(zp3, zp3, pool_w_l, sc_l, nb=1, tt=512, start=0, prev_from_z=True, out_dtype=BF16)
        xp = _outproj(op, pp.reshape(batch * seq, pool_width), xp, w_o, w_p, g_post, tm=512)
        outs["kp"].append(zp3[:, :, attn_w:2 * attn_w].reshape(batch, seq, n_heads, HEAD_W))
        outs["vp"].append(zp3[:, :, 2 * attn_w:3 * attn_w].reshape(batch, seq, n_heads, HEAD_W))
        outs["pp"].append(zp3[:, seq - POOL_STATE:, 4 * attn_w:4 * attn_w + pool_width])

        zs = _inproj(xs, g_pre, w_in_l, cos_s, sin_s, tm=dec_batch * dec_seq, tn=1024)
        os_ = _attn_decode(zs, cache_k, cache_v, page_table, lamv, g_sub, layer=layer,
                           t_new=dec_seq, n_heads=n_heads, lam_init=lam_init)
        zs3 = zs.reshape(dec_batch, dec_seq, zs.shape[1])
        state = state_pool[layer]
        prev = jnp.concatenate([jnp.zeros((dec_batch, 1, pool_width), F32), state], axis=1)
        ps = _pool(zs3, prev, pool_w_l, sc_l, nb=16, tt=dec_seq, start=past_len,
                   prev_from_z=False, out_dtype=F32)
        xs = _outproj(os_, ps.reshape(dec_batch * dec_seq, pool_width), xs, w_o, w_p, g_post, tm=512)
        u_s = zs3[:, :, 4 * attn_w:4 * attn_w + pool_width]
        outs["ks"].append(zs3[:, :, attn_w:2 * attn_w].reshape(dec_batch, dec_seq, n_heads, HEAD_W))
        outs["vs"].append(zs3[:, :, 2 * attn_w:3 * attn_w].reshape(dec_batch, dec_seq, n_heads, HEAD_W))
        outs["ps"].append(jnp.concatenate([state, u_s], axis=1)[:, -POOL_STATE:])

    return (xp.reshape(batch, seq, d_model), xs.reshape(dec_batch, dec_seq, d_model),
            jnp.stack(outs["kp"]), jnp.stack(outs["vp"]), jnp.stack(outs["pp"]),
            jnp.stack(outs["ks"]), jnp.stack(outs["vs"]), jnp.stack(outs["ps"]))
```

```python
import functools
import math

import jax
import jax.numpy as jnp
from jax import lax
from jax.experimental import pallas as pl
from jax.experimental.pallas import tpu as pltpu

F32 = jnp.float32
BF16 = jnp.bfloat16

HEAD_DIM = 64
HEAD_W = 2 * HEAD_DIM
POOL_WINDOWS = (2, 4, 8, 16)
POOL_STATE = max(POOL_WINDOWS) - 1
POOL_PREV = POOL_STATE + 1
ROPE_THETA = 10000.0
EPS = 1e-6
NEG = -1e30
ATTN_SCALE = HEAD_DIM ** -0.5

LANES = 128
SUBLANES = 8
VMEM_LIMIT = 56 * 1024 * 1024


def _params(n_axes):
    return pltpu.CompilerParams(
        dimension_semantics=("arbitrary",) * n_axes, vmem_limit_bytes=VMEM_LIMIT)


def _lambda_init(layer):
    return 0.8 - 0.6 * math.exp(-0.3 * layer)


def _lam_from(lamv, lam_init):
    t1 = jnp.sum(lamv[0:1, :] * lamv[1:2, :], axis=-1, keepdims=True)
    t2 = jnp.sum(lamv[2:3, :] * lamv[3:4, :], axis=-1, keepdims=True)
    return jnp.exp(t1) - jnp.exp(t2) + lam_init


def _silu(x):
    return x * (1.0 / (1.0 + jnp.exp(-x)))


def _head_norm_gate(o, g, lam_init):
    ms = jnp.mean(o * o, axis=-1, keepdims=True)
    return (o * lax.rsqrt(ms + EPS) * g) * (1.0 - lam_init)


def _inproj_kernel(x_ref, g_ref, w_ref, cos_ref, sin_ref, z_ref, h_sc, *, n_rope_sections):
    j = pl.program_id(1)

    @pl.when(j == 0)
    def _():
        x = x_ref[...]
        ms = jnp.mean(x * x, axis=-1, keepdims=True)
        h_sc[...] = ((x * lax.rsqrt(ms + EPS)) * g_ref[...]).astype(BF16)

    acc = jnp.dot(h_sc[...], w_ref[...], preferred_element_type=F32)

    @pl.when(j >= n_rope_sections)
    def _():
        z_ref[...] = acc

    @pl.when(j < n_rope_sections)
    def _():
        cos = cos_ref[...]
        sin = sin_ref[...]
        lane = lax.broadcasted_iota(jnp.int32, cos.shape, 1)
        first_half = (lane % HEAD_DIM) < (HEAD_DIM // 2)
        scale = jnp.where(j == 0, ATTN_SCALE, 1.0).astype(F32)
        for hh in range(acc.shape[1] // LANES):
            blk = acc[:, hh * LANES:(hh + 1) * LANES]
            partner = jnp.where(first_half,
                                pltpu.roll(blk, LANES - HEAD_DIM // 2, 1),
                                pltpu.roll(blk, HEAD_DIM // 2, 1))
            z_ref[:, hh * LANES:(hh + 1) * LANES] = (blk * cos + partner * sin) * scale


def _inproj(x2d, g, w_bf16, cos_t, sin_t, *, tm, tn):
    T, D = x2d.shape
    n_cols = w_bf16.shape[1]
    period_blocks = cos_t.shape[0] // tm
    attn_w = n_cols // 6
    kern = functools.partial(_inproj_kernel, n_rope_sections=2 * attn_w // tn)
    return pl.pallas_call(
        kern,
        grid=(T // tm, n_cols // tn),
        in_specs=[
            pl.BlockSpec((tm, D), lambda i, j: (i, 0)),
            pl.BlockSpec((1, D), lambda i, j: (0, 0)),
            pl.BlockSpec((D, tn), lambda i, j: (0, j)),
            pl.BlockSpec((tm, LANES), lambda i, j: (i % period_blocks, 0)),
            pl.BlockSpec((tm, LANES), lambda i, j: (i % period_blocks, 0)),
        ],
        out_specs=pl.BlockSpec((tm, tn), lambda i, j: (i, j)),
        out_shape=jax.ShapeDtypeStruct((T, n_cols), F32),
        scratch_shapes=[pltpu.VMEM((tm, D), BF16)],
        compiler_params=_params(2),
        name="inproj",
    )(x2d, g, w_bf16, cos_t, sin_t)


ONES_ROWS = 16


def _attn_prompt_kernel(q_ref, k_ref, v_ref, ga_ref, lamv_ref, g_ref, o_ref,
                        s_sc, p_sc, acc_sc, stat_sc, *, lam_init):
    qi = pl.program_id(2)
    tq = q_ref.shape[0]
    q = q_ref[...]
    lane = lax.broadcasted_iota(jnp.int32, q.shape, 1)
    q_maps = [jnp.where((lane >= c * HEAD_DIM) & (lane < (c + 1) * HEAD_DIM), q, 0.0).astype(BF16)
              for c in range(2)]
    nt = (((1,), (1,)), ((), ()))
    ones_rows = jnp.ones((ONES_ROWS, tq), BF16)

    def tile_rows(tile):
        return pl.ds(pl.multiple_of(tile * tq, tq), tq)

    def visit(i):
        return jnp.where(i == 0, qi, i - 1)

    def score_stage(i, slot):
        k = k_ref[tile_rows(visit(i)), :].astype(BF16)
        for c in range(2):
            s_sc[slot, c] = lax.dot_general(k, q_maps[c], nt, preferred_element_type=F32)

    def softmax_stage(slot, on_diag=False):
        for c in range(2):
            m_prev = stat_sc[c:c + 1, :]
            s_t = s_sc[slot, c]
            if on_diag:
                key = lax.broadcasted_iota(jnp.int32, s_t.shape, 0)
                qry = lax.broadcasted_iota(jnp.int32, s_t.shape, 1)
                s_t = jnp.where(key <= qry, s_t, NEG)
            m_new = jnp.maximum(m_prev, jnp.max(s_t, axis=0, keepdims=True))
            p_sc[c] = jnp.exp(s_t - m_new).astype(BF16)
            stat_sc[2 + c:3 + c, :] = jnp.exp(m_prev - m_new)
            stat_sc[c:c + 1, :] = m_new

    def value_stage(i):
        v_t = jnp.concatenate([v_ref[tile_rows(visit(i)), :].T.astype(BF16), ones_rows], axis=0)
        pv = [jnp.dot(v_t, p_sc[c], preferred_element_type=F32) for c in range(2)]
        for c in range(2):
            acc_sc[c] = stat_sc[2 + c:3 + c, :] * acc_sc[c] + pv[c]

    def step(i, slot):
        score_stage(i + 1, 1 - slot)
        value_stage(i - 1)
        softmax_stage(slot)

    def last_step(slot):
        value_stage(qi - 1)
        softmax_stage(slot)

    acc_sc[...] = jnp.zeros(acc_sc.shape, F32)
    stat_sc[...] = jnp.full(stat_sc.shape, NEG, F32)
    score_stage(0, 0)
    score_stage(jnp.minimum(1, qi), 1)
    softmax_stage(0, on_diag=True)

    @pl.loop(1, qi)
    def _(i):
        for slot in range(2):
            @pl.when(i % 2 == slot)
            def _():
                step(i, slot)

    for slot in range(2):
        @pl.when((qi >= 1) & (qi % 2 == slot))
        def _():
            last_step(slot)

    value_stage(qi)

    lam = _lam_from(lamv_ref[...], lam_init)
    acc0, acc1 = acc_sc[0], acc_sc[1]
    o_t = (acc0[:HEAD_W] / acc0[HEAD_W:HEAD_W + 1]
           - lam * (acc1[:HEAD_W] / acc1[HEAD_W:HEAD_W + 1]))
    o = _head_norm_gate(o_t.T, g_ref[...], lam_init)
    o_ref[...] = (o * _silu(ga_ref[...])).astype(o_ref.dtype)


def _attn_prompt(z, lamv, subln_g, *, batch, seq, n_heads, lam_init, tq):
    nq = seq // tq
    attn_w = n_heads * HEAD_W
    kb = attn_w // LANES
    kern = functools.partial(_attn_prompt_kernel, lam_init=lam_init)
    return pl.pallas_call(
        kern,
        grid=(batch, n_heads, nq),
        in_specs=[
            pl.BlockSpec((tq, LANES), lambda b, h, i: (b * nq + i, h)),
            pl.BlockSpec((seq, LANES), lambda b, h, i: (b, kb + h)),
            pl.BlockSpec((seq, LANES), lambda b, h, i: (b, 2 * kb + h)),
            pl.BlockSpec((tq, LANES), lambda b, h, i: (b * nq + i, 3 * kb + h)),
            pl.BlockSpec((4, HEAD_DIM), lambda b, h, i: (0, 0)),
            pl.BlockSpec((1, HEAD_W), lambda b, h, i: (0, 0)),
        ],
        out_specs=pl.BlockSpec((tq, LANES), lambda b, h, i: (b * nq + i, h)),
        out_shape=jax.ShapeDtypeStruct((batch * seq, attn_w), BF16),
        scratch_shapes=[pltpu.VMEM((2, 2, tq, tq), F32), pltpu.VMEM((2, tq, tq), BF16),
                        pltpu.VMEM((2, HEAD_W + ONES_ROWS, tq), F32),
                        pltpu.VMEM((SUBLANES, tq), F32)],
        compiler_params=_params(3),
        name="attn_prompt",
    )(z, z, z, z, lamv, subln_g)


def _attn_decode_kernel(pt_ref, q_ref, kn_ref, vn_ref, ga_ref, lamv_ref, g_ref, *rest,
                        lam_init, n_pages, n_heads):
    del pt_ref
    k_pages = rest[:n_pages]
    v_pages = rest[n_pages:2 * n_pages]
    o_ref = rest[2 * n_pages]
    t_new = q_ref.shape[0]
    page = k_pages[0].shape[0] // n_heads
    half = n_heads // 2
    nt = (((1,), (1,)), ((), ()))

    def pair_rows(pages, h):
        return jnp.concatenate(
            [ref[pl.ds(h, 2 * page, stride=half), :] for ref in pages], axis=0).astype(BF16)

    lam = _lam_from(lamv_ref[...], lam_init)
    g = g_ref[...]
    n_rows = 4 * t_new
    row = lax.broadcasted_iota(jnp.int32, (n_rows, HEAD_W), 0)
    lane = lax.broadcasted_iota(jnp.int32, (n_rows, HEAD_W), 1)
    in_map = (lane // HEAD_DIM) == ((row // t_new) % 2)
    past = 2 * page * n_pages
    row_e = lax.broadcasted_iota(jnp.int32, (n_rows, past), 0) // (2 * t_new)
    own_past = (lax.broadcasted_iota(jnp.int32, (n_rows, past), 1) % 2) == row_e
    r_new = lax.broadcasted_iota(jnp.int32, (n_rows, 2 * t_new), 0)
    c_new = lax.broadcasted_iota(jnp.int32, (n_rows, 2 * t_new), 1)
    own_new = ((c_new // t_new) == (r_new // (2 * t_new))) & ((c_new % t_new) <= (r_new % t_new))

    outs = [None] * n_heads
    for h in range(half):
        cols = [slice((h + e * half) * HEAD_W, (h + e * half + 1) * HEAD_W) for e in range(2)]
        q_rep = jnp.concatenate([q_ref[:, cols[0]]] * 2 + [q_ref[:, cols[1]]] * 2, axis=0)
        q4 = jnp.where(in_map, q_rep, 0.0).astype(BF16)
        k_new = jnp.concatenate([kn_ref[:, cols[0]], kn_ref[:, cols[1]]], axis=0).astype(BF16)
        v_new = jnp.concatenate([vn_ref[:, cols[0]], vn_ref[:, cols[1]]], axis=0).astype(BF16)
        s_past = lax.dot_general(q4, pair_rows(k_pages, h), nt, preferred_element_type=F32)
        s_past = jnp.where(own_past, s_past, NEG)
        s_new = lax.dot_general(q4, k_new, nt, preferred_element_type=F32)
        s_new = jnp.where(own_new, s_new, NEG)
        m = jnp.maximum(jnp.max(s_past, axis=-1, keepdims=True), jnp.max(s_new, axis=-1, keepdims=True))
        p_past = jnp.exp(s_past - m)
        p_new = jnp.exp(s_new - m)
        l = jnp.sum(p_past, axis=-1, keepdims=True) + jnp.sum(p_new, axis=-1, keepdims=True)
        o4 = (jnp.dot(p_past.astype(BF16), pair_rows(v_pages, h), preferred_element_type=F32)
              + jnp.dot(p_new.astype(BF16), v_new, preferred_element_type=F32)) / l
        for e in range(2):
            r0 = e * 2 * t_new
            outs[h + e * half] = _head_norm_gate(
                o4[r0:r0 + t_new] - lam * o4[r0 + t_new:r0 + 2 * t_new], g, lam_init)
    o_ref[...] = jnp.concatenate(outs, axis=1) * _silu(ga_ref[...])


def _attn_decode(z, cache_k, cache_v, page_table, lamv, subln_g, *, layer, t_new, n_heads, lam_init):
    dec_batch, n_pages = page_table.shape
    depth, n_phys, page = cache_k.shape[:3]
    attn_w = n_heads * HEAD_W
    cache_k = cache_k.reshape(depth, n_phys, page * n_heads, HEAD_W)
    cache_v = cache_v.reshape(depth, n_phys, page * n_heads, HEAD_W)
    row_spec = lambda sec: pl.BlockSpec((t_new, attn_w), lambda b, pt: (b, sec))
    page_specs = [
        pl.BlockSpec((None, None, page * n_heads, HEAD_W),
                     lambda b, pt, j=j: (layer, pt[b, j], 0, 0))
        for j in range(n_pages)
    ]
    kern = functools.partial(_attn_decode_kernel, lam_init=lam_init, n_pages=n_pages, n_heads=n_heads)
    grid_spec = pltpu.PrefetchScalarGridSpec(
        num_scalar_prefetch=1,
        grid=(dec_batch,),
        in_specs=[row_spec(0), row_spec(1), row_spec(2), row_spec(3),
                  pl.BlockSpec((4, HEAD_DIM), lambda b, pt: (0, 0)),
                  pl.BlockSpec((1, HEAD_W), lambda b, pt: (0, 0))] + page_specs + page_specs,
        out_specs=pl.BlockSpec((t_new, attn_w), lambda b, pt: (b, 0)),
    )
    return pl.pallas_call(
        kern,
        grid_spec=grid_spec,
        out_shape=jax.ShapeDtypeStruct((dec_batch * t_new, attn_w), F32),
        compiler_params=_params(1),
        name="attn_decode",
    )(page_table, z, z, z, z, lamv, subln_g, *([cache_k] * n_pages), *([cache_v] * n_pages))


def _pool_kernel(u_ref, prev_ref, gp_ref, w_ref, sc_ref, p_ref, ext_sc, *, start, zero_first_prefix):
    i = pl.program_id(1)
    nb, tt, width = u_ref.shape
    gw = width // len(POOL_WINDOWS)
    prev = prev_ref[...]
    if zero_first_prefix:
        prev = jnp.where(i == 0, 0.0, prev)
    u = u_ref[...]
    ext_sc[:, 0:POOL_PREV, :] = prev
    ext_sc[:, POOL_PREV:POOL_PREV + tt, :] = u
    pos = start + i * tt + lax.broadcasted_iota(jnp.int32, (1, tt, 1), 1)
    for g, w in enumerate(POOL_WINDOWS):
        cols = slice(g * gw, (g + 1) * gw)
        ssum = u[:, :, cols]
        for back in range(1, w):
            ssum = ssum + ext_sc[:, POOL_PREV - back:POOL_PREV - back + tt, cols]
        cnt = jnp.minimum(pos + 1, w).astype(F32)
        d = (ssum / cnt - u[:, :, cols]).astype(BF16).reshape(nb * tt, gw)
        y = jnp.dot(d, w_ref[g], preferred_element_type=F32) * sc_ref[:, cols]
        y = y * _silu(gp_ref[:, :, cols].reshape(nb * tt, gw))
        p_ref[:, :, cols] = y.reshape(nb, tt, gw).astype(p_ref.dtype)


def _pool(z3, prev_src, pool_w_bf16, pool_scale, *, nb, tt, start, prev_from_z, out_dtype):
    n_seq, seq, n_cols = z3.shape
    width = n_cols // 6
    u_blk, gp_blk = 4, 5
    if prev_from_z:
        per = tt // POOL_PREV
        prev_spec = pl.BlockSpec((nb, POOL_PREV, width),
                                 lambda b, i: (b, jnp.maximum(i * per - 1, 0), u_blk))
    else:
        prev_spec = pl.BlockSpec((nb, POOL_PREV, width), lambda b, i: (b, 0, 0))
    kern = functools.partial(_pool_kernel, start=start, zero_first_prefix=prev_from_z)
    return pl.pallas_call(
        kern,
        grid=(n_seq // nb, seq // tt),
        in_specs=[
            pl.BlockSpec((nb, tt, width), lambda b, i: (b, i, u_blk)),
            prev_spec,
            pl.BlockSpec((nb, tt, width), lambda b, i: (b, i, gp_blk)),
            pl.BlockSpec(pool_w_bf16.shape, lambda b, i: (0, 0, 0)),
            pl.BlockSpec((1, width), lambda b, i: (0, 0)),
        ],
        out_specs=pl.BlockSpec((nb, tt, width), lambda b, i: (b, i, 0)),
        out_shape=jax.ShapeDtypeStruct((n_seq, seq, width), out_dtype),
        scratch_shapes=[pltpu.VMEM((nb, POOL_PREV + tt, width), F32)],
        compiler_params=_params(2),
        name="pool",
    )(z3, prev_src, z3, pool_w_bf16, pool_scale)


def _outproj_kernel(o_ref, p_ref, x_ref, wo_ref, wp_ref, g_ref, out_ref):
    y = (jnp.dot(o_ref[...].astype(BF16), wo_ref[...], preferred_element_type=F32)
         + jnp.dot(p_ref[...].astype(BF16), wp_ref[...], preferred_element_type=F32))
    ms = jnp.mean(y * y, axis=-1, keepdims=True)
    out_ref[...] = x_ref[...] + (y * lax.rsqrt(ms + EPS)) * g_ref[...]


def _outproj(o, p, x2d, w_o, w_p, g, *, tm):
    T, D = x2d.shape
    half = o.shape[1]
    return pl.pallas_call(
        _outproj_kernel,
        grid=(T // tm,),
        in_specs=[
            pl.BlockSpec((tm, half), lambda i: (i, 0)),
            pl.BlockSpec((tm, half), lambda i: (i, 0)),
            pl.BlockSpec((tm, D), lambda i: (i, 0)),
            pl.BlockSpec((half, D), lambda i: (0, 0)),
            pl.BlockSpec((half, D), lambda i: (0, 0)),
            pl.BlockSpec((1, D), lambda i: (0, 0)),
        ],
        out_specs=pl.BlockSpec((tm, D), lambda i: (i, 0)),
        out_shape=jax.ShapeDtypeStruct((T, D), F32),
        compiler_params=_params(1),
        name="outproj",
    )(o, p, x2d, w_o, w_p, g)


def _rope_tables(pos):
    half = HEAD_DIM // 2
    inv = ROPE_THETA ** (-jnp.arange(half, dtype=F32) * 2.0 / HEAD_DIM)
    ang = pos.astype(F32)[:, None] * inv[None, :]
    reps = LANES // half
    cos = jnp.tile(jnp.cos(ang), (1, reps))
    sin = jnp.tile(jnp.sin(ang), (1, reps))
    lane = jnp.arange(LANES)
    sign = jnp.where((lane % HEAD_DIM) < half, -1.0, 1.0).astype(F32)
    return cos, sin * sign[None, :]


def kernel(x_prompt, x_sample, cache_k, cache_v, state_pool, page_table, norm_pre, norm_post, w_in,
           lambda_q1, lambda_k1, lambda_q2, lambda_k2, subln_g, pool_w, pool_scale, w_out):
    batch, seq, d_model = x_prompt.shape
    dec_batch, dec_seq, _ = x_sample.shape
    depth = w_in.shape[0]
    n_heads = cache_k.shape[3]
    attn_w = n_heads * HEAD_W
    past_len = page_table.shape[1] * cache_k.shape[2]
    pool_width = pool_scale.shape[1]

    cos_p, sin_p = _rope_tables(jnp.arange(seq))
    cos_s, sin_s = _rope_tables(past_len + jnp.arange(dec_seq))
    cos_s = jnp.tile(cos_s, (dec_batch, 1))
    sin_s = jnp.tile(sin_s, (dec_batch, 1))

    xp = x_prompt.reshape(batch * seq, d_model)
    xs = x_sample.reshape(dec_batch * dec_seq, d_model)
    outs = {k: [] for k in ("kp", "vp", "pp", "ks", "vs", "ps")}

    for layer in range(depth):
        lam_init = _lambda_init(layer)
        w_in_l = w_in[layer].astype(BF16)
        w_o = w_out[layer, :attn_w].astype(BF16)
        w_p = w_out[layer, attn_w:].astype(BF16)
        pool_w_l = pool_w[layer].astype(BF16)
        g_pre = norm_pre[layer][None, :]
        g_post = norm_post[layer][None, :]
        g_sub = subln_g[layer][None, :]
        sc_l = pool_scale[layer][None, :]
        lamv = jnp.stack([lambda_q1[layer], lambda_k1[layer], lambda_q2[layer], lambda_k2[layer]])

        zp = _inproj(xp, g_pre, w_in_l, cos_p, sin_p, tm=1024, tn=1024)
        op = _attn_prompt(zp, lamv, g_sub, batch=batch, seq=seq, n_heads=n_heads,
                          lam_init=lam_init, tq=512)
        zp3 = zp.reshape(batch, seq, zp.shape[1])
        pp = _pool(zp3, zp3, pool_w_l, sc_l, nb=1, tt=512, start=0, prev_from_z=True, out_dtype=BF16)
        xp = _outproj(op, pp.reshape(batch * seq, pool_width), xp, w_o, w_p, g_post, tm=512)
        outs["kp"].append(zp3[:, :, attn_w:2 * attn_w].reshape(batch, seq, n_heads, HEAD_W))
        outs["vp"].append(zp3[:, :, 2 * attn_w:3 * attn_w].reshape(batch, seq, n_heads, HEAD_W))
        outs["pp"].append(zp3[:, seq - POOL_STATE:, 4 * attn_w:4 * attn_w + pool_width])

        zs = _inproj(xs, g_pre, w_in_l, cos_s, sin_s, tm=dec_batch * dec_seq, tn=1024)
        os_ = _attn_decode(zs, cache_k, cache_v, page_table, lamv, g_sub, layer=layer,
                           t_new=dec_seq, n_heads=n_heads, lam_init=lam_init)
        zs3 = zs.reshape(dec_batch, dec_seq, zs.shape[1])
        state = state_pool[layer]
        prev = jnp.concatenate([jnp.zeros((dec_batch, 1, pool_width), F32), state], axis=1)
        ps = _pool(zs3, prev, pool_w_l, sc_l, nb=16, tt=dec_seq, start=past_len,
                   prev_from_z=False, out_dtype=F32)
        xs = _outproj(os_, ps.reshape(dec_batch * dec_seq, pool_width), xs, w_o, w_p, g_post, tm=512)
        u_s = zs3[:, :, 4 * attn_w:4 * attn_w + pool_width]
        outs["ks"].append(zs3[:, :, attn_w:2 * attn_w].reshape(dec_batch, dec_seq, n_heads, HEAD_W))
        outs["vs"].append(zs3[:, :, 2 * attn_w:3 * attn_w].reshape(dec_batch, dec_seq, n_heads, HEAD_W))
        outs["ps"].append(jnp.concatenate([state, u_s], axis=1)[:, -POOL_STATE:])

    return (xp.reshape(batch, seq, d_model), xs.reshape(dec_batch, dec_seq, d_model),
            jnp.stack(outs["kp"]), jnp.stack(outs["vp"]), jnp.stack(outs["pp"]),
            jnp.stack(outs["ks"]), jnp.stack(outs["vs"]), jnp.stack(outs["ps"]))
```

```python
import functools
import math

import jax
import jax.numpy as jnp
from jax import lax
from jax.experimental import pallas as pl
from jax.experimental.pallas import tpu as pltpu

F32 = jnp.float32
BF16 = jnp.bfloat16

HEAD_DIM = 64
HEAD_W = 2 * HEAD_DIM
POOL_WINDOWS = (2, 4, 8, 16)
POOL_STATE = max(POOL_WINDOWS) - 1
POOL_PREV = POOL_STATE + 1
ROPE_THETA = 10000.0
EPS = 1e-6
NEG = -1e30
ATTN_SCALE = HEAD_DIM ** -0.5

LANES = 128
SUBLANES = 8
VMEM_LIMIT = 56 * 1024 * 1024


def _params(n_axes):
    return pltpu.CompilerParams(
        dimension_semantics=("arbitrary",) * n_axes, vmem_limit_bytes=VMEM_LIMIT)


def _lambda_init(layer):
    return 0.8 - 0.6 * math.exp(-0.3 * layer)


def _lam_from(lamv, lam_init):
    t1 = jnp.sum(lamv[0:1, :] * lamv[1:2, :], axis=-1, keepdims=True)
    t2 = jnp.sum(lamv[2:3, :] * lamv[3:4, :], axis=-1, keepdims=True)
    return jnp.exp(t1) - jnp.exp(t2) + lam_init


def _silu(x):
    return x * (1.0 / (1.0 + jnp.exp(-x)))


def _head_norm_gate(o, g, lam_init):
    ms = jnp.mean(o * o, axis=-1, keepdims=True)
    return (o * lax.rsqrt(ms + EPS) * g) * (1.0 - lam_init)


MXU_COLS = 256


def _inproj_kernel(x_ref, g_ref, w_ref, cos_ref, sin_ref, *rest, tiles_per_section, n_heads, aliased):
    z_ref, kf_ref, vf_ref, h_sc = rest[2:] if aliased else rest
    j = pl.program_id(1)
    tm, tn = z_ref.shape
    heads_per_tile = tn // HEAD_W

    @pl.when(j == 0)
    def _():
        x = x_ref[...]
        ms = jnp.mean(x * x, axis=-1, keepdims=True)
        h_sc[...] = ((x * lax.rsqrt(ms + EPS)) * g_ref[...]).astype(BF16)

    def rope(acc):
        cos = cos_ref[...]
        sin = sin_ref[...]
        lane = lax.broadcasted_iota(jnp.int32, cos.shape, 1)
        first_half = (lane % HEAD_DIM) < (HEAD_DIM // 2)
        out = []
        for hh in range(acc.shape[1] // LANES):
            blk = acc[:, hh * LANES:(hh + 1) * LANES]
            partner = jnp.where(first_half,
                                pltpu.roll(blk, LANES - HEAD_DIM // 2, 1),
                                pltpu.roll(blk, HEAD_DIM // 2, 1))
            out.append(blk * cos + partner * sin)
        return jnp.concatenate(out, axis=1)

    def tile(epilogue):
        for ch in range(tn // MXU_COLS):
            cols = slice(ch * MXU_COLS, (ch + 1) * MXU_COLS)
            epilogue(ch, cols, jnp.dot(h_sc[...], w_ref[:, cols], preferred_element_type=F32))

    def head_major_store(ref, first_head, val):
        for hh in range(val.shape[1] // HEAD_W):
            ref[pl.ds(first_head + hh, tm, stride=n_heads), :] = val[:, hh * HEAD_W:(hh + 1) * HEAD_W]

    @pl.when(j < tiles_per_section)
    def _():
        def q_epilogue(ch, cols, acc):
            z_ref[:, cols] = rope(acc) * ATTN_SCALE
        tile(q_epilogue)

    for t in range(tiles_per_section):
        @pl.when(j == tiles_per_section + t)
        def _():
            def k_epilogue(ch, cols, acc):
                r = rope(acc)
                z_ref[:, cols] = r
                head_major_store(kf_ref, t * heads_per_tile + ch * (MXU_COLS // HEAD_W), r)
            tile(k_epilogue)

        @pl.when(j == 2 * tiles_per_section + t)
        def _():
            def v_epilogue(ch, cols, acc):
                z_ref[:, cols] = acc
                head_major_store(vf_ref, t * heads_per_tile + ch * (MXU_COLS // HEAD_W), acc)
            tile(v_epilogue)

    @pl.when(j >= 3 * tiles_per_section)
    def _():
        def plain_epilogue(ch, cols, acc):
            z_ref[:, cols] = acc
        tile(plain_epilogue)


def _inproj(x2d, g, w_bf16, cos_t, sin_t, kv_prev, *, layer, depth, n_heads, tm, tn):
    T, D = x2d.shape
    n_cols = w_bf16.shape[1]
    period_blocks = cos_t.shape[0] // tm
    attn_w = n_cols // 6
    aliased = kv_prev is not None
    kern = functools.partial(_inproj_kernel, tiles_per_section=attn_w // tn, n_heads=n_heads,
                             aliased=aliased)
    kv_shape = jax.ShapeDtypeStruct((depth, T * n_heads, HEAD_W), F32)
    kv_spec = pl.BlockSpec((None, tm * n_heads, HEAD_W), lambda i, j: (layer, i, 0))
    in_specs = [
        pl.BlockSpec((tm, D), lambda i, j: (i, 0)),
        pl.BlockSpec((1, D), lambda i, j: (0, 0)),
        pl.BlockSpec((D, tn), lambda i, j: (0, j)),
        pl.BlockSpec((tm, LANES), lambda i, j: (i % period_blocks, 0)),
        pl.BlockSpec((tm, LANES), lambda i, j: (i % period_blocks, 0)),
    ]
    args = [x2d, g, w_bf16, cos_t, sin_t]
    if aliased:
        in_specs += [pl.BlockSpec(memory_space=pl.ANY)] * 2
        args += list(kv_prev)
    z, kf, vf = pl.pallas_call(
        kern,
        grid=(T // tm, n_cols // tn),
        in_specs=in_specs,
        out_specs=[pl.BlockSpec((tm, tn), lambda i, j: (i, j)), kv_spec, kv_spec],
        out_shape=[jax.ShapeDtypeStruct((T, n_cols), F32), kv_shape, kv_shape],
        scratch_shapes=[pltpu.VMEM((tm, D), BF16)],
        input_output_aliases={5: 1, 6: 2} if aliased else {},
        compiler_params=_params(2),
        name="inproj",
    )(*args)
    return z, (kf, vf)


ONES_ROWS = 16


def _attn_prompt_kernel(q_ref, k_ref, v_ref, ga_ref, lamv_ref, g_ref, o_ref,
                        s_sc, p_sc, acc_sc, stat_sc, *, lam_init):
    qi = pl.program_id(2)
    tq = q_ref.shape[0]
    q = q_ref[...]
    lane = lax.broadcasted_iota(jnp.int32, q.shape, 1)
    q_maps = [jnp.where((lane >= c * HEAD_DIM) & (lane < (c + 1) * HEAD_DIM), q, 0.0).astype(BF16)
              for c in range(2)]
    nt = (((1,), (1,)), ((), ()))
    ones_rows = jnp.ones((ONES_ROWS, tq), BF16)

    def tile_rows(tile):
        return pl.ds(pl.multiple_of(tile * tq, tq), tq)

    def visit(i):
        return jnp.where(i == 0, qi, i - 1)

    def score_stage(i, slot):
        k = k_ref[tile_rows(visit(i)), :].astype(BF16)
        for c in range(2):
            s_sc[slot, c] = lax.dot_general(k, q_maps[c], nt, preferred_element_type=F32)

    def softmax_stage(slot, on_diag=False):
        for c in range(2):
            m_prev = stat_sc[c:c + 1, :]
            s_t = s_sc[slot, c]
            if on_diag:
                key = lax.broadcasted_iota(jnp.int32, s_t.shape, 0)
                qry = lax.broadcasted_iota(jnp.int32, s_t.shape, 1)
                s_t = jnp.where(key <= qry, s_t, NEG)
            m_new = jnp.maximum(m_prev, jnp.max(s_t, axis=0, keepdims=True))
            p_sc[c] = jnp.exp(s_t - m_new).astype(BF16)
            stat_sc[2 + c:3 + c, :] = jnp.exp(m_prev - m_new)
            stat_sc[c:c + 1, :] = m_new

    def value_stage(i):
        v_t = jnp.concatenate([v_ref[tile_rows(visit(i)), :].T.astype(BF16), ones_rows], axis=0)
        pv = [jnp.dot(v_t, p_sc[c], preferred_element_type=F32) for c in range(2)]
        for c in range(2):
            acc_sc[c] = stat_sc[2 + c:3 + c, :] * acc_sc[c] + pv[c]

    def step(i, slot):
        score_stage(i + 1, 1 - slot)
        value_stage(i - 1)
        softmax_stage(slot)

    def last_step(slot):
        value_stage(qi - 1)
        softmax_stage(slot)

    acc_sc[...] = jnp.zeros(acc_sc.shape, F32)
    stat_sc[...] = jnp.full(stat_sc.shape, NEG, F32)
    score_stage(0, 0)
    score_stage(jnp.minimum(1, qi), 1)
    softmax_stage(0, on_diag=True)

    @pl.loop(1, qi)
    def _(i):
        for slot in range(2):
            @pl.when(i % 2 == slot)
            def _():
                step(i, slot)

    for slot in range(2):
        @pl.when((qi >= 1) & (qi % 2 == slot))
        def _():
            last_step(slot)

    value_stage(qi)

    lam = _lam_from(lamv_ref[...], lam_init)
    acc0, acc1 = acc_sc[0], acc_sc[1]
    o_t = (acc0[:HEAD_W] / acc0[HEAD_W:HEAD_W + 1]
           - lam * (acc1[:HEAD_W] / acc1[HEAD_W:HEAD_W + 1]))
    o = _head_norm_gate(o_t.T, g_ref[...], lam_init)
    o_ref[...] = (o * _silu(ga_ref[...])).astype(o_ref.dtype)


def _attn_prompt(z, lamv, subln_g, *, batch, seq, n_heads, lam_init, tq):
    nq = seq // tq
    attn_w = n_heads * HEAD_W
    kb = attn_w // LANES
    kern = functools.partial(_attn_prompt_kernel, lam_init=lam_init)
    return pl.pallas_call(
        kern,
        grid=(batch, n_heads, nq),
        in_specs=[
            pl.BlockSpec((tq, LANES), lambda b, h, i: (b * nq + i, h)),
            pl.BlockSpec((seq, LANES), lambda b, h, i: (b, kb + h)),
            pl.BlockSpec((seq, LANES), lambda b, h, i: (b, 2 * kb + h)),
            pl.BlockSpec((tq, LANES), lambda b, h, i: (b * nq + i, 3 * kb + h)),
            pl.BlockSpec((4, HEAD_DIM), lambda b, h, i: (0, 0)),
            pl.BlockSpec((1, HEAD_W), lambda b, h, i: (0, 0)),
        ],
        out_specs=pl.BlockSpec((tq, LANES), lambda b, h, i: (b * nq + i, h)),
        out_shape=jax.ShapeDtypeStruct((batch * seq, attn_w), BF16),
        scratch_shapes=[pltpu.VMEM((2, 2, tq, tq), F32), pltpu.VMEM((2, tq, tq), BF16),
                        pltpu.VMEM((2, HEAD_W + ONES_ROWS, tq), F32),
                        pltpu.VMEM((SUBLANES, tq), F32)],
        compiler_params=_params(3),
        name="attn_prompt",
    )(z, z, z, z, lamv, subln_g)


def _attn_decode_kernel(pt_ref, q_ref, kn_ref, vn_ref, ga_ref, lamv_ref, g_ref, *rest,
                        lam_init, n_pages, n_heads):
    del pt_ref
    k_pages = rest[:n_pages]
    v_pages = rest[n_pages:2 * n_pages]
    o_ref = rest[2 * n_pages]
    t_new = q_ref.shape[0]
    page = k_pages[0].shape[0] // n_heads
    half = n_heads // 2
    nt = (((1,), (1,)), ((), ()))

    def pair_rows(pages, h):
        return jnp.concatenate(
            [ref[pl.ds(h, 2 * page, stride=half), :] for ref in pages], axis=0).astype(BF16)

    lam = _lam_from(lamv_ref[...], lam_init)
    g = g_ref[...]
    n_rows = 4 * t_new
    row = lax.broadcasted_iota(jnp.int32, (n_rows, HEAD_W), 0)
    lane = lax.broadcasted_iota(jnp.int32, (n_rows, HEAD_W), 1)
    in_map = (lane // HEAD_DIM) == ((row // t_new) % 2)
    past = 2 * page * n_pages
    row_e = lax.broadcasted_iota(jnp.int32, (n_rows, past), 0) // (2 * t_new)
    own_past = (lax.broadcasted_iota(jnp.int32, (n_rows, past), 1) % 2) == row_e
    r_new = lax.broadcasted_iota(jnp.int32, (n_rows, 2 * t_new), 0)
    c_new = lax.broadcasted_iota(jnp.int32, (n_rows, 2 * t_new), 1)
    own_new = ((c_new // t_new) == (r_new // (2 * t_new))) & ((c_new % t_new) <= (r_new % t_new))

    outs = [None] * n_heads
    for h in range(half):
        cols = [slice((h + e * half) * HEAD_W, (h + e * half + 1) * HEAD_W) for e in range(2)]
        q_rep = jnp.concatenate([q_ref[:, cols[0]]] * 2 + [q_ref[:, cols[1]]] * 2, axis=0)
        q4 = jnp.where(in_map, q_rep, 0.0).astype(BF16)
        k_new = jnp.concatenate([kn_ref[:, cols[0]], kn_ref[:, cols[1]]], axis=0).astype(BF16)
        v_new = jnp.concatenate([vn_ref[:, cols[0]], vn_ref[:, cols[1]]], axis=0).astype(BF16)
        s_past = lax.dot_general(q4, pair_rows(k_pages, h), nt, preferred_element_type=F32)
        s_past = jnp.where(own_past, s_past, NEG)
        s_new = lax.dot_general(q4, k_new, nt, preferred_element_type=F32)
        s_new = jnp.where(own_new, s_new, NEG)
        m = jnp.maximum(jnp.max(s_past, axis=-1, keepdims=True), jnp.max(s_new, axis=-1, keepdims=True))
        p_past = jnp.exp(s_past - m)
        p_new = jnp.exp(s_new - m)
        l = jnp.sum(p_past, axis=-1, keepdims=True) + jnp.sum(p_new, axis=-1, keepdims=True)
        o4 = (jnp.dot(p_past.astype(BF16), pair_rows(v_pages, h), preferred_element_type=F32)
              + jnp.dot(p_new.astype(BF16), v_new, preferred_element_type=F32)) / l
        for e in range(2):
            r0 = e * 2 * t_new
            outs[h + e * half] = _head_norm_gate(
                o4[r0:r0 + t_new] - lam * o4[r0 + t_new:r0 + 2 * t_new], g, lam_init)
    o_ref[...] = jnp.concatenate(outs, axis=1) * _silu(ga_ref[...])


def _attn_decode(z, cache_k, cache_v, page_table, lamv, subln_g, *, layer, t_new, n_heads, lam_init):
    dec_batch, n_pages = page_table.shape
    depth, n_phys, page = cache_k.shape[:3]
    attn_w = n_heads * HEAD_W
    cache_k = cache_k.reshape(depth, n_phys, page * n_heads, HEAD_W)
    cache_v = cache_v.reshape(depth, n_phys, page * n_heads, HEAD_W)
    row_spec = lambda sec: pl.BlockSpec((t_new, attn_w), lambda b, pt: (b, sec))
    page_specs = [
        pl.BlockSpec((None, None, page * n_heads, HEAD_W),
                     lambda b, pt, j=j: (layer, pt[b, j], 0, 0))
        for j in range(n_pages)
    ]
    kern = functools.partial(_attn_decode_kernel, lam_init=lam_init, n_pages=n_pages, n_heads=n_heads)
    grid_spec = pltpu.PrefetchScalarGridSpec(
        num_scalar_prefetch=1,
        grid=(dec_batch,),
        in_specs=[row_spec(0), row_spec(1), row_spec(2), row_spec(3),
                  pl.BlockSpec((4, HEAD_DIM), lambda b, pt: (0, 0)),
                  pl.BlockSpec((1, HEAD_W), lambda b, pt: (0, 0))] + page_specs + page_specs,
        out_specs=pl.BlockSpec((t_new, attn_w), lambda b, pt: (b, 0)),
    )
    return pl.pallas_call(
        kern,
        grid_spec=grid_spec,
        out_shape=jax.ShapeDtypeStruct((dec_batch * t_new, attn_w), F32),
        compiler_params=_params(1),
        name="attn_decode",
    )(page_table, z, z, z, z, lamv, subln_g, *([cache_k] * n_pages), *([cache_v] * n_pages))


def _pool_kernel(u_ref, prev_ref, gp_ref, w_ref, sc_ref, p_ref, ext_sc, *, start, zero_first_prefix):
    i = pl.program_id(1)
    nb, tt, width = u_ref.shape
    gw = width // len(POOL_WINDOWS)
    prev = prev_ref[...]
    if zero_first_prefix:
        prev = jnp.where(i == 0, 0.0, prev)
    u = u_ref[...]
    ext_sc[:, 0:POOL_PREV, :] = prev
    ext_sc[:, POOL_PREV:POOL_PREV + tt, :] = u
    pos = start + i * tt + lax.broadcasted_iota(jnp.int32, (1, tt, 1), 1)
    for g, w in enumerate(POOL_WINDOWS):
        cols = slice(g * gw, (g + 1) * gw)
        ssum = u[:, :, cols]
        for back in range(1, w):
            ssum = ssum + ext_sc[:, POOL_PREV - back:POOL_PREV - back + tt, cols]
        cnt = jnp.minimum(pos + 1, w).astype(F32)
        d = (ssum / cnt - u[:, :, cols]).astype(BF16).reshape(nb * tt, gw)
        y = jnp.dot(d, w_ref[g], preferred_element_type=F32) * sc_ref[:, cols]
        y = y * _silu(gp_ref[:, :, cols].reshape(nb * tt, gw))
        p_ref[:, :, cols] = y.reshape(nb, tt, gw).astype(p_ref.dtype)


def _pool(z3, prev_src, pool_w_bf16, pool_scale, *, nb, tt, start, prev_from_z, out_dtype):
    n_seq, seq, n_cols = z3.shape
    width = n_cols // 6
    u_blk, gp_blk = 4, 5
    if prev_from_z:
        per = tt // POOL_PREV
        prev_spec = pl.BlockSpec((nb, POOL_PREV, width),
                                 lambda b, i: (b, jnp.maximum(i * per - 1, 0), u_blk))
    else:
        prev_spec = pl.BlockSpec((nb, POOL_PREV, width), lambda b, i: (b, 0, 0))
    kern = functools.partial(_pool_kernel, start=start, zero_first_prefix=prev_from_z)
    return pl.pallas_call(
        kern,
        grid=(n_seq // nb, seq // tt),
        in_specs=[
            pl.BlockSpec((nb, tt, width), lambda b, i: (b, i, u_blk)),
            prev_spec,
            pl.BlockSpec((nb, tt, width), lambda b, i: (b, i, gp_blk)),
            pl.BlockSpec(pool_w_bf16.shape, lambda b, i: (0, 0, 0)),
            pl.BlockSpec((1, width), lambda b, i: (0, 0)),
        ],
        out_specs=pl.BlockSpec((nb, tt, width), lambda b, i: (b, i, 0)),
        out_shape=jax.ShapeDtypeStruct((n_seq, seq, width), out_dtype),
        scratch_shapes=[pltpu.VMEM((nb, POOL_PREV + tt, width), F32)],
        compiler_params=_params(2),
        name="pool",
    )(z3, prev_src, z3, pool_w_bf16, pool_scale)


def _outproj_kernel(o_ref, p_ref, x_ref, wo_ref, wp_ref, g_ref, out_ref):
    y = (jnp.dot(o_ref[...].astype(BF16), wo_ref[...], preferred_element_type=F32)
         + jnp.dot(p_ref[...].astype(BF16), wp_ref[...], preferred_element_type=F32))
    ms = jnp.mean(y * y, axis=-1, keepdims=True)
    out_ref[...] = x_ref[...] + (y * lax.rsqrt(ms + EPS)) * g_ref[...]


def _outproj(o, p, x2d, w_o, w_p, g, *, tm):
    T, D = x2d.shape
    half = o.shape[1]
    return pl.pallas_call(
        _outproj_kernel,
        grid=(T // tm,),
        in_specs=[
            pl.BlockSpec((tm, half), lambda i: (i, 0)),
            pl.BlockSpec((tm, half), lambda i: (i, 0)),
            pl.BlockSpec((tm, D), lambda i: (i, 0)),
            pl.BlockSpec((half, D), lambda i: (0, 0)),
            pl.BlockSpec((half, D), lambda i: (0, 0)),
            pl.BlockSpec((1, D), lambda i: (0, 0)),
        ],
        out_specs=pl.BlockSpec((tm, D), lambda i: (i, 0)),
        out_shape=jax.ShapeDtypeStruct((T, D), F32),
        compiler_params=_params(1),
        name="outproj",
    )(o, p, x2d, w_o, w_p, g)


def _rope_tables(pos):
    half = HEAD_DIM // 2
    inv = ROPE_THETA ** (-jnp.arange(half, dtype=F32) * 2.0 / HEAD_DIM)
    ang = pos.astype(F32)[:, None] * inv[None, :]
    reps = LANES // half
    cos = jnp.tile(jnp.cos(ang), (1, reps))
    sin = jnp.tile(jnp.sin(ang), (1, reps))
    lane = jnp.arange(LANES)
    sign = jnp.where((lane % HEAD_DIM) < half, -1.0, 1.0).astype(F32)
    return cos, sin * sign[None, :]


def kernel(x_prompt, x_sample, cache_k, cache_v, state_pool, page_table, norm_pre, norm_post, w_in,
           lambda_q1, lambda_k1, lambda_q2, lambda_k2, subln_g, pool_w, pool_scale, w_out):
    batch, seq, d_model = x_prompt.shape
    dec_batch, dec_seq, _ = x_sample.shape
    depth = w_in.shape[0]
    n_heads = cache_k.shape[3]
    attn_w = n_heads * HEAD_W
    past_len = page_table.shape[1] * cache_k.shape[2]
    pool_width = pool_scale.shape[1]

    cos_p, sin_p = _rope_tables(jnp.arange(seq))
    cos_s, sin_s = _rope_tables(past_len + jnp.arange(dec_seq))
    cos_s = jnp.tile(cos_s, (dec_batch, 1))
    sin_s = jnp.tile(sin_s, (dec_batch, 1))

    xp = x_prompt.reshape(batch * seq, d_model)
    xs = x_sample.reshape(dec_batch * dec_seq, d_model)
    outs = {k: [] for k in ("pp", "ps")}
    kv_p = kv_s = None

    for layer in range(depth):
        lam_init = _lambda_init(layer)
        w_in_l = w_in[layer].astype(BF16)
        w_o = w_out[layer, :attn_w].astype(BF16)
        w_p = w_out[layer, attn_w:].astype(BF16)
        pool_w_l = pool_w[layer].astype(BF16)
        g_pre = norm_pre[layer][None, :]
        g_post = norm_post[layer][None, :]
        g_sub = subln_g[layer][None, :]
        sc_l = pool_scale[layer][None, :]
        lamv = jnp.stack([lambda_q1[layer], lambda_k1[layer], lambda_q2[layer], lambda_k2[layer]])

        zp, kv_p = _inproj(xp, g_pre, w_in_l, cos_p, sin_p, kv_p, layer=layer, depth=depth,
                           n_heads=n_heads, tm=1024, tn=512)
        op = _attn_prompt(zp, lamv, g_sub, batch=batch, seq=seq, n_heads=n_heads,
                          lam_init=lam_init, tq=512)
        zp3 = zp.reshape(batch, seq, zp.shape[1])
        pp = _pool(zp3, zp3, pool_w_l, sc_l, nb=1, tt=512, start=0, prev_from_z=True, out_dtype=BF16)
        xp = _outproj(op, pp.reshape(batch * seq, pool_width), xp, w_o, w_p, g_post, tm=512)
        outs["pp"].append(zp3[:, seq - POOL_STATE:, 4 * attn_w:4 * attn_w + pool_width])

        zs, kv_s = _inproj(xs, g_pre, w_in_l, cos_s, sin_s, kv_s, layer=layer, depth=depth,
                           n_heads=n_heads, tm=dec_batch * dec_seq, tn=512)
        os_ = _attn_decode(zs, cache_k, cache_v, page_table, lamv, g_sub, layer=layer,
                           t_new=dec_seq, n_heads=n_heads, lam_init=lam_init)
        zs3 = zs.reshape(dec_batch, dec_seq, zs.shape[1])
        state = state_pool[layer]
        prev = jnp.concatenate([jnp.zeros((dec_batch, 1, pool_width), F32), state], axis=1)
        ps = _pool(zs3, prev, pool_w_l, sc_l, nb=16, tt=dec_seq, start=past_len,
                   prev_from_z=False, out_dtype=F32)
        xs = _outproj(os_, ps.reshape(dec_batch * dec_seq, pool_width), xs, w_o, w_p, g_post, tm=512)
        u_s = zs3[:, :, 4 * attn_w:4 * attn_w + pool_width]
        outs["ps"].append(jnp.concatenate([state, u_s], axis=1)[:, -POOL_STATE:])

    return (xp.reshape(batch, seq, d_model), xs.reshape(dec_batch, dec_seq, d_model),
            kv_p[0].reshape(depth, batch, seq, n_heads, HEAD_W),
            kv_p[1].reshape(depth, batch, seq, n_heads, HEAD_W), jnp.stack(outs["pp"]),
            kv_s[0].reshape(depth, dec_batch, dec_seq, n_heads, HEAD_W),
            kv_s[1].reshape(depth, dec_batch, dec_seq, n_heads, HEAD_W), jnp.stack(outs["ps"]))
```

```python
import functools
import math

import jax
import jax.numpy as jnp
from jax import lax
from jax.experimental import pallas as pl
from jax.experimental.pallas import tpu as pltpu

F32 = jnp.float32
BF16 = jnp.bfloat16

HEAD_DIM = 64
HEAD_W = 2 * HEAD_DIM
POOL_WINDOWS = (2, 4, 8, 16)
POOL_STATE = max(POOL_WINDOWS) - 1
POOL_PREV = POOL_STATE + 1
ROPE_THETA = 10000.0
EPS = 1e-6
NEG = -1e30
ATTN_SCALE = HEAD_DIM ** -0.5

LANES = 128
SUBLANES = 8
VMEM_LIMIT = 56 * 1024 * 1024


def _params(n_axes):
    return pltpu.CompilerParams(
        dimension_semantics=("arbitrary",) * n_axes, vmem_limit_bytes=VMEM_LIMIT)


def _lambda_init(layer):
    return 0.8 - 0.6 * math.exp(-0.3 * layer)


def _lam_from(lamv, lam_init):
    t1 = jnp.sum(lamv[0:1, :] * lamv[1:2, :], axis=-1, keepdims=True)
    t2 = jnp.sum(lamv[2:3, :] * lamv[3:4, :], axis=-1, keepdims=True)
    return jnp.exp(t1) - jnp.exp(t2) + lam_init


def _silu(x):
    return x * (1.0 / (1.0 + jnp.exp(-x)))


def _head_norm_gate(o, g, lam_init):
    ms = jnp.mean(o * o, axis=-1, keepdims=True)
    return (o * lax.rsqrt(ms + EPS) * g) * (1.0 - lam_init)


MXU_COLS = 256


def _inproj_kernel(x_ref, g_ref, w_ref, cos_ref, sin_ref, *rest, tiles_per_section, n_heads, aliased):
    z_ref, kf_ref, vf_ref, h_sc = rest[2:] if aliased else rest
    j = pl.program_id(1)
    tm, tn = z_ref.shape
    heads_per_tile = tn // HEAD_W

    @pl.when(j == 0)
    def _():
        x = x_ref[...]
        ms = jnp.mean(x * x, axis=-1, keepdims=True)
        h_sc[...] = ((x * lax.rsqrt(ms + EPS)) * g_ref[...]).astype(BF16)

    def rope(acc):
        cos = cos_ref[...]
        sin = sin_ref[...]
        lane = lax.broadcasted_iota(jnp.int32, cos.shape, 1)
        first_half = (lane % HEAD_DIM) < (HEAD_DIM // 2)
        out = []
        for hh in range(acc.shape[1] // LANES):
            blk = acc[:, hh * LANES:(hh + 1) * LANES]
            partner = jnp.where(first_half,
                                pltpu.roll(blk, LANES - HEAD_DIM // 2, 1),
                                pltpu.roll(blk, HEAD_DIM // 2, 1))
            out.append(blk * cos + partner * sin)
        return jnp.concatenate(out, axis=1)

    def tile(epilogue):
        for ch in range(tn // MXU_COLS):
            cols = slice(ch * MXU_COLS, (ch + 1) * MXU_COLS)
            epilogue(ch, cols, jnp.dot(h_sc[...], w_ref[:, cols].astype(BF16),
                                        preferred_element_type=F32))

    def head_major_store(ref, first_head, val):
        for hh in range(val.shape[1] // HEAD_W):
            ref[pl.ds(first_head + hh, tm, stride=n_heads), :] = val[:, hh * HEAD_W:(hh + 1) * HEAD_W]

    @pl.when(j < tiles_per_section)
    def _():
        def q_epilogue(ch, cols, acc):
            z_ref[:, cols] = rope(acc) * ATTN_SCALE
        tile(q_epilogue)

    for t in range(tiles_per_section):
        @pl.when(j == tiles_per_section + t)
        def _():
            def k_epilogue(ch, cols, acc):
                r = rope(acc)
                z_ref[:, cols] = r
                head_major_store(kf_ref, t * heads_per_tile + ch * (MXU_COLS // HEAD_W), r)
            tile(k_epilogue)

        @pl.when(j == 2 * tiles_per_section + t)
        def _():
            def v_epilogue(ch, cols, acc):
                z_ref[:, cols] = acc
                head_major_store(vf_ref, t * heads_per_tile + ch * (MXU_COLS // HEAD_W), acc)
            tile(v_epilogue)

    @pl.when(j >= 3 * tiles_per_section)
    def _():
        def plain_epilogue(ch, cols, acc):
            z_ref[:, cols] = acc
        tile(plain_epilogue)


def _inproj(x2d, g, w_in, cos_t, sin_t, kv_prev, *, layer, n_heads, tm, tn):
    T, D = x2d.shape
    depth, _, n_cols = w_in.shape
    period_blocks = cos_t.shape[0] // tm
    attn_w = n_cols // 6
    aliased = kv_prev is not None
    kern = functools.partial(_inproj_kernel, tiles_per_section=attn_w // tn, n_heads=n_heads,
                             aliased=aliased)
    kv_shape = jax.ShapeDtypeStruct((depth, T * n_heads, HEAD_W), F32)
    kv_spec = pl.BlockSpec((None, tm * n_heads, HEAD_W), lambda i, j: (layer, i, 0))
    in_specs = [
        pl.BlockSpec((tm, D), lambda i, j: (i, 0)),
        pl.BlockSpec((1, D), lambda i, j: (0, 0)),
        pl.BlockSpec((None, D, tn), lambda i, j: (layer, 0, j)),
        pl.BlockSpec((tm, LANES), lambda i, j: (i % period_blocks, 0)),
        pl.BlockSpec((tm, LANES), lambda i, j: (i % period_blocks, 0)),
    ]
    args = [x2d, g, w_in, cos_t, sin_t]
    if aliased:
        in_specs += [pl.BlockSpec(memory_space=pl.ANY)] * 2
        args += list(kv_prev)
    z, kf, vf = pl.pallas_call(
        kern,
        grid=(T // tm, n_cols // tn),
        in_specs=in_specs,
        out_specs=[pl.BlockSpec((tm, tn), lambda i, j: (i, j)), kv_spec, kv_spec],
        out_shape=[jax.ShapeDtypeStruct((T, n_cols), F32), kv_shape, kv_shape],
        scratch_shapes=[pltpu.VMEM((tm, D), BF16)],
        input_output_aliases={5: 1, 6: 2} if aliased else {},
        compiler_params=_params(2),
        name="inproj",
    )(*args)
    return z, (kf, vf)


ONES_ROWS = 16


def _attn_prompt_kernel(q_ref, k_ref, v_ref, ga_ref, lamv_ref, g_ref, o_ref,
                        s_sc, p_sc, acc_sc, stat_sc, *, lam_init):
    qi = pl.program_id(2)
    tq = q_ref.shape[0]
    q = q_ref[...]
    lane = lax.broadcasted_iota(jnp.int32, q.shape, 1)
    q_maps = [jnp.where((lane >= c * HEAD_DIM) & (lane < (c + 1) * HEAD_DIM), q, 0.0).astype(BF16)
              for c in range(2)]
    nt = (((1,), (1,)), ((), ()))
    ones_rows = jnp.ones((ONES_ROWS, tq), BF16)

    def tile_rows(tile):
        return pl.ds(pl.multiple_of(tile * tq, tq), tq)

    def visit(i):
        return jnp.where(i == 0, qi, i - 1)

    def score_stage(i, slot):
        k = k_ref[tile_rows(visit(i)), :].astype(BF16)
        for c in range(2):
            s_sc[slot, c] = lax.dot_general(k, q_maps[c], nt, preferred_element_type=F32)

    def softmax_stage(slot, on_diag=False):
        for c in range(2):
            m_prev = stat_sc[c:c + 1, :]
            s_t = s_sc[slot, c]
            if on_diag:
                key = lax.broadcasted_iota(jnp.int32, s_t.shape, 0)
                qry = lax.broadcasted_iota(jnp.int32, s_t.shape, 1)
                s_t = jnp.where(key <= qry, s_t, NEG)
            m_new = jnp.maximum(m_prev, jnp.max(s_t, axis=0, keepdims=True))
            p_sc[c] = jnp.exp(s_t - m_new).astype(BF16)
            stat_sc[2 + c:3 + c, :] = jnp.exp(m_prev - m_new)
            stat_sc[c:c + 1, :] = m_new

    def value_stage(i):
        v_t = jnp.concatenate([v_ref[tile_rows(visit(i)), :].T.astype(BF16), ones_rows], axis=0)
        pv = [jnp.dot(v_t, p_sc[c], preferred_element_type=F32) for c in range(2)]
        for c in range(2):
            acc_sc[c] = stat_sc[2 + c:3 + c, :] * acc_sc[c] + pv[c]

    def step(i, slot):
        score_stage(i + 1, 1 - slot)
        value_stage(i - 1)
        softmax_stage(slot)

    def last_step(slot):
        value_stage(qi - 1)
        softmax_stage(slot)

    acc_sc[...] = jnp.zeros(acc_sc.shape, F32)
    stat_sc[...] = jnp.full(stat_sc.shape, NEG, F32)
    score_stage(0, 0)
    score_stage(jnp.minimum(1, qi), 1)
    softmax_stage(0, on_diag=True)

    @pl.loop(1, qi)
    def _(i):
        for slot in range(2):
            @pl.when(i % 2 == slot)
            def _():
                step(i, slot)

    for slot in range(2):
        @pl.when((qi >= 1) & (qi % 2 == slot))
        def _():
            last_step(slot)

    value_stage(qi)

    lam = _lam_from(lamv_ref[...], lam_init)
    acc0, acc1 = acc_sc[0], acc_sc[1]
    o_t = (acc0[:HEAD_W] / acc0[HEAD_W:HEAD_W + 1]
           - lam * (acc1[:HEAD_W] / acc1[HEAD_W:HEAD_W + 1]))
    o = _head_norm_gate(o_t.T, g_ref[...], lam_init)
    o_ref[...] = (o * _silu(ga_ref[...])).astype(o_ref.dtype)


def _attn_prompt(z, lamv, subln_g, *, batch, seq, n_heads, lam_init, tq):
    nq = seq // tq
    attn_w = n_heads * HEAD_W
    kb = attn_w // LANES
    kern = functools.partial(_attn_prompt_kernel, lam_init=lam_init)
    return pl.pallas_call(
        kern,
        grid=(batch, n_heads, nq),
        in_specs=[
            pl.BlockSpec((tq, LANES), lambda b, h, i: (b * nq + i, h)),
            pl.BlockSpec((seq, LANES), lambda b, h, i: (b, kb + h)),
            pl.BlockSpec((seq, LANES), lambda b, h, i: (b, 2 * kb + h)),
            pl.BlockSpec((tq, LANES), lambda b, h, i: (b * nq + i, 3 * kb + h)),
            pl.BlockSpec((4, HEAD_DIM), lambda b, h, i: (0, 0)),
            pl.BlockSpec((1, HEAD_W), lambda b, h, i: (0, 0)),
        ],
        out_specs=pl.BlockSpec((tq, LANES), lambda b, h, i: (b * nq + i, h)),
        out_shape=jax.ShapeDtypeStruct((batch * seq, attn_w), BF16),
        scratch_shapes=[pltpu.VMEM((2, 2, tq, tq), F32), pltpu.VMEM((2, tq, tq), BF16),
                        pltpu.VMEM((2, HEAD_W + ONES_ROWS, tq), F32),
                        pltpu.VMEM((SUBLANES, tq), F32)],
        compiler_params=_params(3),
        name="attn_prompt",
    )(z, z, z, z, lamv, subln_g)


def _attn_decode_kernel(pt_ref, q_ref, kn_ref, vn_ref, ga_ref, lamv_ref, g_ref, *rest,
                        lam_init, n_pages, n_heads):
    del pt_ref
    k_pages = rest[:n_pages]
    v_pages = rest[n_pages:2 * n_pages]
    o_ref = rest[2 * n_pages]
    t_new = q_ref.shape[0]
    page = k_pages[0].shape[0] // n_heads
    half = n_heads // 2
    nt = (((1,), (1,)), ((), ()))

    def pair_rows(pages, h):
        return jnp.concatenate(
            [ref[pl.ds(h, 2 * page, stride=half), :] for ref in pages], axis=0).astype(BF16)

    lam = _lam_from(lamv_ref[...], lam_init)
    g = g_ref[...]
    n_rows = 4 * t_new
    row = lax.broadcasted_iota(jnp.int32, (n_rows, HEAD_W), 0)
    lane = lax.broadcasted_iota(jnp.int32, (n_rows, HEAD_W), 1)
    in_map = (lane // HEAD_DIM) == ((row // t_new) % 2)
    past = 2 * page * n_pages
    row_e = lax.broadcasted_iota(jnp.int32, (n_rows, past), 0) // (2 * t_new)
    own_past = (lax.broadcasted_iota(jnp.int32, (n_rows, past), 1) % 2) == row_e
    r_new = lax.broadcasted_iota(jnp.int32, (n_rows, 2 * t_new), 0)
    c_new = lax.broadcasted_iota(jnp.int32, (n_rows, 2 * t_new), 1)
    own_new = ((c_new // t_new) == (r_new // (2 * t_new))) & ((c_new % t_new) <= (r_new % t_new))

    outs = [None] * n_heads
    for h in range(half):
        cols = [slice((h + e * half) * HEAD_W, (h + e * half + 1) * HEAD_W) for e in range(2)]
        q_rep = jnp.concatenate([q_ref[:, cols[0]]] * 2 + [q_ref[:, cols[1]]] * 2, axis=0)
        q4 = jnp.where(in_map, q_rep, 0.0).astype(BF16)
        k_new = jnp.concatenate([kn_ref[:, cols[0]], kn_ref[:, cols[1]]], axis=0).astype(BF16)
        v_new = jnp.concatenate([vn_ref[:, cols[0]], vn_ref[:, cols[1]]], axis=0).astype(BF16)
        s_past = lax.dot_general(q4, pair_rows(k_pages, h), nt, preferred_element_type=F32)
        s_past = jnp.where(own_past, s_past, NEG)
        s_new = lax.dot_general(q4, k_new, nt, preferred_element_type=F32)
        s_new = jnp.where(own_new, s_new, NEG)
        m = jnp.maximum(jnp.max(s_past, axis=-1, keepdims=True), jnp.max(s_new, axis=-1, keepdims=True))
        p_past = jnp.exp(s_past - m)
        p_new = jnp.exp(s_new - m)
        l = jnp.sum(p_past, axis=-1, keepdims=True) + jnp.sum(p_new, axis=-1, keepdims=True)
        o4 = (jnp.dot(p_past.astype(BF16), pair_rows(v_pages, h), preferred_element_type=F32)
              + jnp.dot(p_new.astype(BF16), v_new, preferred_element_type=F32)) / l
        for e in range(2):
            r0 = e * 2 * t_new
            outs[h + e * half] = _head_norm_gate(
                o4[r0:r0 + t_new] - lam * o4[r0 + t_new:r0 + 2 * t_new], g, lam_init)
    o_ref[...] = jnp.concatenate(outs, axis=1) * _silu(ga_ref[...])


def _attn_decode(z, cache_k, cache_v, page_table, lamv, subln_g, *, layer, t_new, n_heads, lam_init):
    dec_batch, n_pages = page_table.shape
    depth, n_phys, page = cache_k.shape[:3]
    attn_w = n_heads * HEAD_W
    cache_k = cache_k.reshape(depth, n_phys, page * n_heads, HEAD_W)
    cache_v = cache_v.reshape(depth, n_phys, page * n_heads, HEAD_W)
    row_spec = lambda sec: pl.BlockSpec((t_new, attn_w), lambda b, pt: (b, sec))
    page_specs = [
        pl.BlockSpec((None, None, page * n_heads, HEAD_W),
                     lambda b, pt, j=j: (layer, pt[b, j], 0, 0))
        for j in range(n_pages)
    ]
    kern = functools.partial(_attn_decode_kernel, lam_init=lam_init, n_pages=n_pages, n_heads=n_heads)
    grid_spec = pltpu.PrefetchScalarGridSpec(
        num_scalar_prefetch=1,
        grid=(dec_batch,),
        in_specs=[row_spec(0), row_spec(1), row_spec(2), row_spec(3),
                  pl.BlockSpec((4, HEAD_DIM), lambda b, pt: (0, 0)),
                  pl.BlockSpec((1, HEAD_W), lambda b, pt: (0, 0))] + page_specs + page_specs,
        out_specs=pl.BlockSpec((t_new, attn_w), lambda b, pt: (b, 0)),
    )
    return pl.pallas_call(
        kern,
        grid_spec=grid_spec,
        out_shape=jax.ShapeDtypeStruct((dec_batch * t_new, attn_w), F32),
        compiler_params=_params(1),
        name="attn_decode",
    )(page_table, z, z, z, z, lamv, subln_g, *([cache_k] * n_pages), *([cache_v] * n_pages))


def _pool_kernel(u_ref, prev_ref, gp_ref, w_ref, sc_ref, p_ref, state_ref, ext_sc, *,
                 start, zero_first_prefix):
    i = pl.program_id(1)
    nb, tt, width = u_ref.shape
    gw = width // len(POOL_WINDOWS)
    n_prev = prev_ref.shape[1]
    prev = prev_ref[...]
    if zero_first_prefix:
        prev = jnp.where(i == 0, 0.0, prev)
    u = u_ref[...]
    ext_sc[:, POOL_PREV - n_prev:POOL_PREV, :] = prev
    ext_sc[:, POOL_PREV:POOL_PREV + tt, :] = u
    pos = start + i * tt + lax.broadcasted_iota(jnp.int32, (1, tt, 1), 1)
    for g, w in enumerate(POOL_WINDOWS):
        cols = slice(g * gw, (g + 1) * gw)
        ssum = u[:, :, cols]
        for back in range(1, w):
            ssum = ssum + ext_sc[:, POOL_PREV - back:POOL_PREV - back + tt, cols]
        cnt = jnp.minimum(pos + 1, w).astype(F32)
        d = (ssum / cnt - u[:, :, cols]).astype(BF16).reshape(nb * tt, gw)
        y = jnp.dot(d, w_ref[g], preferred_element_type=F32) * sc_ref[:, cols]
        y = y * _silu(gp_ref[:, :, cols].reshape(nb * tt, gw))
        p_ref[:, :, cols] = y.reshape(nb, tt, gw).astype(p_ref.dtype)

    @pl.when(i == pl.num_programs(1) - 1)
    def _():
        state_ref[...] = ext_sc[:, tt + POOL_PREV - POOL_STATE:tt + POOL_PREV, :]


def _pool(z3, state, pool_w_bf16, pool_scale, *, layer, nb, tt, start, out_dtype):
    n_seq, seq, n_cols = z3.shape
    width = n_cols // 6
    u_blk, gp_blk = 4, 5
    if state is None:
        per = tt // POOL_PREV
        prev_src = z3
        prev_spec = pl.BlockSpec((nb, POOL_PREV, width),
                                 lambda b, i: (b, jnp.maximum(i * per - 1, 0), u_blk))
    else:
        prev_src = state
        prev_spec = pl.BlockSpec((None, nb, POOL_STATE, width), lambda b, i: (layer, b, 0, 0))
    kern = functools.partial(_pool_kernel, start=start, zero_first_prefix=state is None)
    return pl.pallas_call(
        kern,
        grid=(n_seq // nb, seq // tt),
        in_specs=[
            pl.BlockSpec((nb, tt, width), lambda b, i: (b, i, u_blk)),
            prev_spec,
            pl.BlockSpec((nb, tt, width), lambda b, i: (b, i, gp_blk)),
            pl.BlockSpec(pool_w_bf16.shape, lambda b, i: (0, 0, 0)),
            pl.BlockSpec((1, width), lambda b, i: (0, 0)),
        ],
        out_specs=[pl.BlockSpec((nb, tt, width), lambda b, i: (b, i, 0)),
                   pl.BlockSpec((nb, POOL_STATE, width), lambda b, i: (b, 0, 0))],
        out_shape=[jax.ShapeDtypeStruct((n_seq, seq, width), out_dtype),
                   jax.ShapeDtypeStruct((n_seq, POOL_STATE, width), F32)],
        scratch_shapes=[pltpu.VMEM((nb, POOL_PREV + tt, width), F32)],
        compiler_params=_params(2),
        name="pool",
    )(z3, prev_src, z3, pool_w_bf16, pool_scale)


def _outproj_kernel(o_ref, p_ref, x_ref, wo_ref, wp_ref, g_ref, out_ref):
    y = (jnp.dot(o_ref[...].astype(BF16), wo_ref[...], preferred_element_type=F32)
         + jnp.dot(p_ref[...].astype(BF16), wp_ref[...], preferred_element_type=F32))
    ms = jnp.mean(y * y, axis=-1, keepdims=True)
    out_ref[...] = x_ref[...] + (y * lax.rsqrt(ms + EPS)) * g_ref[...]


def _outproj(o, p, x2d, w_o, w_p, g, *, tm):
    T, D = x2d.shape
    half = o.shape[1]
    return pl.pallas_call(
        _outproj_kernel,
        grid=(T // tm,),
        in_specs=[
            pl.BlockSpec((tm, half), lambda i: (i, 0)),
            pl.BlockSpec((tm, half), lambda i: (i, 0)),
            pl.BlockSpec((tm, D), lambda i: (i, 0)),
            pl.BlockSpec((half, D), lambda i: (0, 0)),
            pl.BlockSpec((half, D), lambda i: (0, 0)),
            pl.BlockSpec((1, D), lambda i: (0, 0)),
        ],
        out_specs=pl.BlockSpec((tm, D), lambda i: (i, 0)),
        out_shape=jax.ShapeDtypeStruct((T, D), F32),
        compiler_params=_params(1),
        name="outproj",
    )(o, p, x2d, w_o, w_p, g)


def _rope_tables(pos):
    half = HEAD_DIM // 2
    inv = ROPE_THETA ** (-jnp.arange(half, dtype=F32) * 2.0 / HEAD_DIM)
    ang = pos.astype(F32)[:, None] * inv[None, :]
    reps = LANES // half
    cos = jnp.tile(jnp.cos(ang), (1, reps))
    sin = jnp.tile(jnp.sin(ang), (1, reps))
    lane = jnp.arange(LANES)
    sign = jnp.where((lane % HEAD_DIM) < half, -1.0, 1.0).astype(F32)
    return cos, sin * sign[None, :]


def kernel(x_prompt, x_sample, cache_k, cache_v, state_pool, page_table, norm_pre, norm_post, w_in,
           lambda_q1, lambda_k1, lambda_q2, lambda_k2, subln_g, pool_w, pool_scale, w_out):
    batch, seq, d_model = x_prompt.shape
    dec_batch, dec_seq, _ = x_sample.shape
    depth = w_in.shape[0]
    n_heads = cache_k.shape[3]
    attn_w = n_heads * HEAD_W
    past_len = page_table.shape[1] * cache_k.shape[2]
    pool_width = pool_scale.shape[1]

    cos_p, sin_p = _rope_tables(jnp.arange(seq))
    cos_s, sin_s = _rope_tables(past_len + jnp.arange(dec_seq))
    cos_s = jnp.tile(cos_s, (dec_batch, 1))
    sin_s = jnp.tile(sin_s, (dec_batch, 1))

    xp = x_prompt.reshape(batch * seq, d_model)
    xs = x_sample.reshape(dec_batch * dec_seq, d_model)
    outs = {k: [] for k in ("pp", "ps")}
    kv_p = kv_s = None

    for layer in range(depth):
        lam_init = _lambda_init(layer)
        w_o = w_out[layer, :attn_w].astype(BF16)
        w_p = w_out[layer, attn_w:].astype(BF16)
        pool_w_l = pool_w[layer].astype(BF16)
        g_pre = norm_pre[layer][None, :]
        g_post = norm_post[layer][None, :]
        g_sub = subln_g[layer][None, :]
        sc_l = pool_scale[layer][None, :]
        lamv = jnp.stack([lambda_q1[layer], lambda_k1[layer], lambda_q2[layer], lambda_k2[layer]])

        zp, kv_p = _inproj(xp, g_pre, w_in, cos_p, sin_p, kv_p, layer=layer, n_heads=n_heads,
                           tm=1024, tn=512)
        op = _attn_prompt(zp, lamv, g_sub, batch=batch, seq=seq, n_heads=n_heads,
                          lam_init=lam_init, tq=512)
        zp3 = zp.reshape(batch, seq, zp.shape[1])
        pp, pp_state = _pool(zp3, None, pool_w_l, sc_l, layer=layer, nb=1, tt=512, start=0,
                             out_dtype=BF16)
        xp = _outproj(op, pp.reshape(batch * seq, pool_width), xp, w_o, w_p, g_post, tm=512)
        outs["pp"].append(pp_state)

        zs, kv_s = _inproj(xs, g_pre, w_in, cos_s, sin_s, kv_s, layer=layer, n_heads=n_heads,
                           tm=dec_batch * dec_seq, tn=512)
        os_ = _attn_decode(zs, cache_k, cache_v, page_table, lamv, g_sub, layer=layer,
                           t_new=dec_seq, n_heads=n_heads, lam_init=lam_init)
        zs3 = zs.reshape(dec_batch, dec_seq, zs.shape[1])
        ps, ps_state = _pool(zs3, state_pool, pool_w_l, sc_l, layer=layer, nb=16, tt=dec_seq,
                             start=past_len, out_dtype=F32)
        xs = _outproj(os_, ps.reshape(dec_batch * dec_seq, pool_width), xs, w_o, w_p, g_post, tm=512)
        outs["ps"].append(ps_state)

    return (xp.reshape(batch, seq, d_model), xs.reshape(dec_batch, dec_seq, d_model),
            kv_p[0].reshape(depth, batch, seq, n_heads, HEAD_W),
            kv_p[1].reshape(depth, batch, seq, n_heads, HEAD_W), jnp.stack(outs["pp"]),
            kv_s[0].reshape(depth, dec_batch, dec_seq, n_heads, HEAD_W),
            kv_s[1].reshape(depth, dec_batch, dec_seq, n_heads, HEAD_W), jnp.stack(outs["ps"]))
```

```python
import functools
import math

import jax
import jax.numpy as jnp
from jax import lax
from jax.experimental import pallas as pl
from jax.experimental.pallas import tpu as pltpu

F32 = jnp.float32
BF16 = jnp.bfloat16

HEAD_DIM = 64
HEAD_W = 2 * HEAD_DIM
POOL_WINDOWS = (2, 4, 8, 16)
POOL_STATE = max(POOL_WINDOWS) - 1
POOL_PREV = POOL_STATE + 1
ROPE_THETA = 10000.0
EPS = 1e-6
NEG = -1e30
ATTN_SCALE = HEAD_DIM ** -0.5

LANES = 128
SUBLANES = 8
VMEM_LIMIT = 56 * 1024 * 1024


def _params(n_axes):
    return pltpu.CompilerParams(
        dimension_semantics=("arbitrary",) * n_axes, vmem_limit_bytes=VMEM_LIMIT)


def _lambda_init(layer):
    return 0.8 - 0.6 * math.exp(-0.3 * layer)


def _lam_from(lamv, lam_init):
    t1 = jnp.sum(lamv[0:1, :] * lamv[1:2, :], axis=-1, keepdims=True)
    t2 = jnp.sum(lamv[2:3, :] * lamv[3:4, :], axis=-1, keepdims=True)
    return jnp.exp(t1) - jnp.exp(t2) + lam_init


def _silu(x):
    return x * (1.0 / (1.0 + jnp.exp(-x)))


def _head_norm_gate(o, g, lam_init):
    ms = jnp.mean(o * o, axis=-1, keepdims=True)
    return (o * lax.rsqrt(ms + EPS) * g) * (1.0 - lam_init)


MXU_COLS = 256


def _inproj_kernel(x_ref, g_ref, w_ref, cos_ref, sin_ref, *rest, tiles_per_section, n_heads, aliased):
    z_ref, kf_ref, vf_ref, h_sc = rest[2:] if aliased else rest
    j = pl.program_id(1)
    tm, tn = z_ref.shape
    heads_per_tile = tn // HEAD_W

    @pl.when(j == 0)
    def _():
        x = x_ref[...]
        ms = jnp.mean(x * x, axis=-1, keepdims=True)
        h_sc[...] = ((x * lax.rsqrt(ms + EPS)) * g_ref[...]).astype(BF16)

    def rope(acc):
        cos = cos_ref[...]
        sin = sin_ref[...]
        lane = lax.broadcasted_iota(jnp.int32, cos.shape, 1)
        first_half = (lane % HEAD_DIM) < (HEAD_DIM // 2)
        out = []
        for hh in range(acc.shape[1] // LANES):
            blk = acc[:, hh * LANES:(hh + 1) * LANES]
            partner = jnp.where(first_half,
                                pltpu.roll(blk, LANES - HEAD_DIM // 2, 1),
                                pltpu.roll(blk, HEAD_DIM // 2, 1))
            out.append(blk * cos + partner * sin)
        return jnp.concatenate(out, axis=1)

    def tile(epilogue):
        for ch in range(tn // MXU_COLS):
            cols = slice(ch * MXU_COLS, (ch + 1) * MXU_COLS)
            epilogue(ch, cols, jnp.dot(h_sc[...], w_ref[:, cols].astype(BF16),
                                        preferred_element_type=F32))

    def head_major_store(ref, first_head, val):
        for hh in range(val.shape[1] // HEAD_W):
            ref[pl.ds(first_head + hh, tm, stride=n_heads), :] = val[:, hh * HEAD_W:(hh + 1) * HEAD_W]

    @pl.when(j < tiles_per_section)
    def _():
        def q_epilogue(ch, cols, acc):
            z_ref[:, cols] = rope(acc) * ATTN_SCALE
        tile(q_epilogue)

    for t in range(tiles_per_section):
        @pl.when(j == tiles_per_section + t)
        def _():
            def k_epilogue(ch, cols, acc):
                r = rope(acc)
                z_ref[:, cols] = r
                head_major_store(kf_ref, t * heads_per_tile + ch * (MXU_COLS // HEAD_W), r)
            tile(k_epilogue)

        @pl.when(j == 2 * tiles_per_section + t)
        def _():
            def v_epilogue(ch, cols, acc):
                z_ref[:, cols] = acc
                head_major_store(vf_ref, t * heads_per_tile + ch * (MXU_COLS // HEAD_W), acc)
            tile(v_epilogue)

    @pl.when(j >= 3 * tiles_per_section)
    def _():
        def plain_epilogue(ch, cols, acc):
            z_ref[:, cols] = acc
        tile(plain_epilogue)


def _inproj(x2d, g, w_in, cos_t, sin_t, kv_prev, *, layer, n_heads, tm, tn):
    T, D = x2d.shape
    depth, _, n_cols = w_in.shape
    period_blocks = cos_t.shape[0] // tm
    attn_w = n_cols // 6
    aliased = kv_prev is not None
    kern = functools.partial(_inproj_kernel, tiles_per_section=attn_w // tn, n_heads=n_heads,
                             aliased=aliased)
    kv_shape = jax.ShapeDtypeStruct((depth, T * n_heads, HEAD_W), F32)
    kv_spec = pl.BlockSpec((None, tm * n_heads, HEAD_W), lambda i, j: (layer, i, 0))
    in_specs = [
        pl.BlockSpec((tm, D), lambda i, j: (i, 0)),
        pl.BlockSpec((1, D), lambda i, j: (0, 0)),
        pl.BlockSpec((None, D, tn), lambda i, j: (layer, 0, j)),
        pl.BlockSpec((tm, LANES), lambda i, j: (i % period_blocks, 0)),
        pl.BlockSpec((tm, LANES), lambda i, j: (i % period_blocks, 0)),
    ]
    args = [x2d, g, w_in, cos_t, sin_t]
    if aliased:
        in_specs += [pl.BlockSpec(memory_space=pl.ANY)] * 2
        args += list(kv_prev)
    z, kf, vf = pl.pallas_call(
        kern,
        grid=(T // tm, n_cols // tn),
        in_specs=in_specs,
        out_specs=[pl.BlockSpec((tm, tn), lambda i, j: (i, j)), kv_spec, kv_spec],
        out_shape=[jax.ShapeDtypeStruct((T, n_cols), F32), kv_shape, kv_shape],
        scratch_shapes=[pltpu.VMEM((tm, D), BF16)],
        input_output_aliases={5: 1, 6: 2} if aliased else {},
        compiler_params=_params(2),
        name="inproj",
    )(*args)
    return z, (kf, vf)


ONES_ROWS = 16


def _prompt_attention(qi, q_ref, k_ref, v_ref, ga_ref, lamv_ref, g_ref, o_ref,
                      s_sc, p_sc, acc_sc, stat_sc, *, lam_init):
    tq = q_ref.shape[0]
    q = q_ref[...]
    lane = lax.broadcasted_iota(jnp.int32, q.shape, 1)
    q_maps = [jnp.where((lane >= c * HEAD_DIM) & (lane < (c + 1) * HEAD_DIM), q, 0.0).astype(BF16)
              for c in range(2)]
    nt = (((1,), (1,)), ((), ()))
    ones_rows = jnp.ones((ONES_ROWS, tq), BF16)

    def tile_rows(tile):
        return pl.ds(pl.multiple_of(tile * tq, tq), tq)

    def visit(i):
        return jnp.where(i == 0, qi, i - 1)

    def score_stage(i, slot):
        k = k_ref[tile_rows(visit(i)), :].astype(BF16)
        for c in range(2):
            s_sc[slot, c] = lax.dot_general(k, q_maps[c], nt, preferred_element_type=F32)

    def softmax_stage(slot, on_diag=False):
        for c in range(2):
            m_prev = stat_sc[c:c + 1, :]
            s_t = s_sc[slot, c]
            if on_diag:
                key = lax.broadcasted_iota(jnp.int32, s_t.shape, 0)
                qry = lax.broadcasted_iota(jnp.int32, s_t.shape, 1)
                s_t = jnp.where(key <= qry, s_t, NEG)
            m_new = jnp.maximum(m_prev, jnp.max(s_t, axis=0, keepdims=True))
            p_sc[c] = jnp.exp(s_t - m_new).astype(BF16)
            stat_sc[2 + c:3 + c, :] = jnp.exp(m_prev - m_new)
            stat_sc[c:c + 1, :] = m_new

    def value_stage(i):
        v_t = jnp.concatenate([v_ref[tile_rows(visit(i)), :].T.astype(BF16), ones_rows], axis=0)
        pv = [jnp.dot(v_t, p_sc[c], preferred_element_type=F32) for c in range(2)]
        for c in range(2):
            acc_sc[c] = stat_sc[2 + c:3 + c, :] * acc_sc[c] + pv[c]

    def step(i, slot):
        score_stage(i + 1, 1 - slot)
        value_stage(i - 1)
        softmax_stage(slot)

    def last_step(slot):
        value_stage(qi - 1)
        softmax_stage(slot)

    acc_sc[...] = jnp.zeros(acc_sc.shape, F32)
    stat_sc[...] = jnp.full(stat_sc.shape, NEG, F32)
    score_stage(0, 0)
    score_stage(jnp.minimum(1, qi), 1)
    softmax_stage(0, on_diag=True)

    @pl.loop(1, qi)
    def _(i):
        for slot in range(2):
            @pl.when(i % 2 == slot)
            def _():
                step(i, slot)

    for slot in range(2):
        @pl.when((qi >= 1) & (qi % 2 == slot))
        def _():
            last_step(slot)

    value_stage(qi)

    lam = _lam_from(lamv_ref[...], lam_init)
    acc0, acc1 = acc_sc[0], acc_sc[1]
    o_t = (acc0[:HEAD_W] / acc0[HEAD_W:HEAD_W + 1]
           - lam * (acc1[:HEAD_W] / acc1[HEAD_W:HEAD_W + 1]))
    o = _head_norm_gate(o_t.T, g_ref[...], lam_init)
    o_ref[...] = (o * _silu(ga_ref[...])).astype(o_ref.dtype)


def _decode_attention(q_ref, kn_ref, vn_ref, ga_ref, lamv_ref, g_ref, k_pages, v_pages, o_ref, *,
                      lam_init, n_heads):
    n_pages = len(k_pages)
    t_new = q_ref.shape[0]
    page = k_pages[0].shape[0] // n_heads
    half = n_heads // 2
    nt = (((1,), (1,)), ((), ()))

    def pair_rows(pages, h):
        return jnp.concatenate(
            [ref[pl.ds(h, 2 * page, stride=half), :] for ref in pages], axis=0).astype(BF16)

    lam = _lam_from(lamv_ref[...], lam_init)
    g = g_ref[...]
    n_rows = 4 * t_new
    row = lax.broadcasted_iota(jnp.int32, (n_rows, HEAD_W), 0)
    lane = lax.broadcasted_iota(jnp.int32, (n_rows, HEAD_W), 1)
    in_map = (lane // HEAD_DIM) == ((row // t_new) % 2)
    past = 2 * page * n_pages
    row_e = lax.broadcasted_iota(jnp.int32, (n_rows, past), 0) // (2 * t_new)
    own_past = (lax.broadcasted_iota(jnp.int32, (n_rows, past), 1) % 2) == row_e
    r_new = lax.broadcasted_iota(jnp.int32, (n_rows, 2 * t_new), 0)
    c_new = lax.broadcasted_iota(jnp.int32, (n_rows, 2 * t_new), 1)
    own_new = ((c_new // t_new) == (r_new // (2 * t_new))) & ((c_new % t_new) <= (r_new % t_new))

    outs = [None] * n_heads
    for h in range(half):
        cols = [slice((h + e * half) * HEAD_W, (h + e * half + 1) * HEAD_W) for e in range(2)]
        q_rep = jnp.concatenate([q_ref[:, cols[0]]] * 2 + [q_ref[:, cols[1]]] * 2, axis=0)
        q4 = jnp.where(in_map, q_rep, 0.0).astype(BF16)
        k_new = jnp.concatenate([kn_ref[:, cols[0]], kn_ref[:, cols[1]]], axis=0).astype(BF16)
        v_new = jnp.concatenate([vn_ref[:, cols[0]], vn_ref[:, cols[1]]], axis=0).astype(BF16)
        s_past = lax.dot_general(q4, pair_rows(k_pages, h), nt, preferred_element_type=F32)
        s_past = jnp.where(own_past, s_past, NEG)
        s_new = lax.dot_general(q4, k_new, nt, preferred_element_type=F32)
        s_new = jnp.where(own_new, s_new, NEG)
        m = jnp.maximum(jnp.max(s_past, axis=-1, keepdims=True), jnp.max(s_new, axis=-1, keepdims=True))
        p_past = jnp.exp(s_past - m)
        p_new = jnp.exp(s_new - m)
        l = jnp.sum(p_past, axis=-1, keepdims=True) + jnp.sum(p_new, axis=-1, keepdims=True)
        o4 = (jnp.dot(p_past.astype(BF16), pair_rows(v_pages, h), preferred_element_type=F32)
              + jnp.dot(p_new.astype(BF16), v_new, preferred_element_type=F32)) / l
        for e in range(2):
            r0 = e * 2 * t_new
            outs[h + e * half] = _head_norm_gate(
                o4[r0:r0 + t_new] - lam * o4[r0 + t_new:r0 + 2 * t_new], g, lam_init)
    o_ref[...] = jnp.concatenate(outs, axis=1) * _silu(ga_ref[...])


def _attention_kernel(pt_ref, qp_ref, kp_ref, vp_ref, gap_ref, qs_ref, kns_ref, vns_ref, gas_ref,
                      lamv_ref, g_ref, *rest, lam_init, n_pages, n_heads):
    del pt_ref
    k_pages = rest[:n_pages]
    v_pages = rest[n_pages:2 * n_pages]
    op_ref, os_ref = rest[2 * n_pages:2 * n_pages + 2]
    scratch = rest[2 * n_pages + 2:]
    _decode_attention(qs_ref, kns_ref, vns_ref, gas_ref, lamv_ref, g_ref, k_pages, v_pages, os_ref,
                      lam_init=lam_init, n_heads=n_heads)
    _prompt_attention(pl.program_id(2), qp_ref, kp_ref, vp_ref, gap_ref, lamv_ref, g_ref, op_ref,
                      *scratch, lam_init=lam_init)


def _attention(zp, zs, cache_k, cache_v, page_table, lamv, subln_g, *, layer, batch, seq, t_new,
               n_heads, lam_init, tq):
    nq = seq // tq
    dec_batch, n_pages = page_table.shape
    assert dec_batch == batch * n_heads * nq, (dec_batch, batch, n_heads, nq)
    depth, n_phys, page = cache_k.shape[:3]
    attn_w = n_heads * HEAD_W
    kb = attn_w // LANES
    cache_k = cache_k.reshape(depth, n_phys, page * n_heads, HEAD_W)
    cache_v = cache_v.reshape(depth, n_phys, page * n_heads, HEAD_W)

    def seq_of(b, h, i):
        return (b * n_heads + h) * nq + i

    row_spec = lambda sec: pl.BlockSpec((t_new, attn_w), lambda b, h, i, pt: (seq_of(b, h, i), sec))
    page_specs = [
        pl.BlockSpec((None, None, page * n_heads, HEAD_W),
                     lambda b, h, i, pt, j=j: (layer, pt[seq_of(b, h, i), j], 0, 0))
        for j in range(n_pages)
    ]
    kern = functools.partial(_attention_kernel, lam_init=lam_init, n_pages=n_pages, n_heads=n_heads)
    grid_spec = pltpu.PrefetchScalarGridSpec(
        num_scalar_prefetch=1,
        grid=(batch, n_heads, nq),
        in_specs=[
            pl.BlockSpec((tq, LANES), lambda b, h, i, pt: (b * nq + i, h)),
            pl.BlockSpec((seq, LANES), lambda b, h, i, pt: (b, kb + h)),
            pl.BlockSpec((seq, LANES), lambda b, h, i, pt: (b, 2 * kb + h)),
            pl.BlockSpec((tq, LANES), lambda b, h, i, pt: (b * nq + i, 3 * kb + h)),
            row_spec(0), row_spec(1), row_spec(2), row_spec(3),
            pl.BlockSpec((4, HEAD_DIM), lambda b, h, i, pt: (0, 0)),
            pl.BlockSpec((1, HEAD_W), lambda b, h, i, pt: (0, 0)),
        ] + page_specs + page_specs,
        out_specs=[
            pl.BlockSpec((tq, LANES), lambda b, h, i, pt: (b * nq + i, h)),
            pl.BlockSpec((t_new, attn_w), lambda b, h, i, pt: (seq_of(b, h, i), 0)),
        ],
        scratch_shapes=[pltpu.VMEM((2, 2, tq, tq), F32), pltpu.VMEM((2, tq, tq), BF16),
                        pltpu.VMEM((2, HEAD_W + ONES_ROWS, tq), F32),
                        pltpu.VMEM((SUBLANES, tq), F32)],
    )
    return pl.pallas_call(
        kern,
        grid_spec=grid_spec,
        out_shape=[jax.ShapeDtypeStruct((batch * seq, attn_w), BF16),
                   jax.ShapeDtypeStruct((dec_batch * t_new, attn_w), F32)],
        compiler_params=_params(3),
        name="attention",
    )(page_table, zp, zp, zp, zp, zs, zs, zs, zs, lamv, subln_g,
      *([cache_k] * n_pages), *([cache_v] * n_pages))


def _pool_kernel(u_ref, prev_ref, gp_ref, w_ref, sc_ref, p_ref, state_ref, ext_sc, *,
                 start, zero_first_prefix):
    i = pl.program_id(1)
    nb, tt, width = u_ref.shape
    gw = width // len(POOL_WINDOWS)
    n_prev = prev_ref.shape[1]
    prev = prev_ref[...]
    if zero_first_prefix:
        prev = jnp.where(i == 0, 0.0, prev)
    u = u_ref[...]
    ext_sc[:, POOL_PREV - n_prev:POOL_PREV, :] = prev
    ext_sc[:, POOL_PREV:POOL_PREV + tt, :] = u
    pos = start + i * tt + lax.broadcasted_iota(jnp.int32, (1, tt, 1), 1)
    for g, w in enumerate(POOL_WINDOWS):
        cols = slice(g * gw, (g + 1) * gw)
        ssum = u[:, :, cols]
        for back in range(1, w):
            ssum = ssum + ext_sc[:, POOL_PREV - back:POOL_PREV - back + tt, cols]
        cnt = jnp.minimum(pos + 1, w).astype(F32)
        d = (ssum / cnt - u[:, :, cols]).astype(BF16).reshape(nb * tt, gw)
        y = jnp.dot(d, w_ref[g], preferred_element_type=F32) * sc_ref[:, cols]
        y = y * _silu(gp_ref[:, :, cols].reshape(nb * tt, gw))
        p_ref[:, :, cols] = y.reshape(nb, tt, gw).astype(p_ref.dtype)

    @pl.when(i == pl.num_programs(1) - 1)
    def _():
        state_ref[...] = ext_sc[:, tt + POOL_PREV - POOL_STATE:tt + POOL_PREV, :]


def _pool(z3, state, pool_w_bf16, pool_scale, *, layer, nb, tt, start, out_dtype):
    n_seq, seq, n_cols = z3.shape
    width = n_cols // 6
    u_blk, gp_blk = 4, 5
    if state is None:
        per = tt // POOL_PREV
        prev_src = z3
        prev_spec = pl.BlockSpec((nb, POOL_PREV, width),
                                 lambda b, i: (b, jnp.maximum(i * per - 1, 0), u_blk))
    else:
        prev_src = state
        prev_spec = pl.BlockSpec((None, nb, POOL_STATE, width), lambda b, i: (layer, b, 0, 0))
    kern = functools.partial(_pool_kernel, start=start, zero_first_prefix=state is None)
    return pl.pallas_call(
        kern,
        grid=(n_seq // nb, seq // tt),
        in_specs=[
            pl.BlockSpec((nb, tt, width), lambda b, i: (b, i, u_blk)),
            prev_spec,
            pl.BlockSpec((nb, tt, width), lambda b, i: (b, i, gp_blk)),
            pl.BlockSpec(pool_w_bf16.shape, lambda b, i: (0, 0, 0)),
            pl.BlockSpec((1, width), lambda b, i: (0, 0)),
        ],
        out_specs=[pl.BlockSpec((nb, tt, width), lambda b, i: (b, i, 0)),
                   pl.BlockSpec((nb, POOL_STATE, width), lambda b, i: (b, 0, 0))],
        out_shape=[jax.ShapeDtypeStruct((n_seq, seq, width), out_dtype),
                   jax.ShapeDtypeStruct((n_seq, POOL_STATE, width), F32)],
        scratch_shapes=[pltpu.VMEM((nb, POOL_PREV + tt, width), F32)],
        compiler_params=_params(2),
        name="pool",
    )(z3, prev_src, z3, pool_w_bf16, pool_scale)


def _outproj_kernel(o_ref, p_ref, x_ref, wo_ref, wp_ref, g_ref, out_ref):
    y = (jnp.dot(o_ref[...].astype(BF16), wo_ref[...], preferred_element_type=F32)
         + jnp.dot(p_ref[...].astype(BF16), wp_ref[...], preferred_element_type=F32))
    ms = jnp.mean(y * y, axis=-1, keepdims=True)
    out_ref[...] = x_ref[...] + (y * lax.rsqrt(ms + EPS)) * g_ref[...]


def _outproj(o, p, x2d, w_o, w_p, g, *, tm):
    T, D = x2d.shape
    half = o.shape[1]
    return pl.pallas_call(
        _outproj_kernel,
        grid=(T // tm,),
        in_specs=[
            pl.BlockSpec((tm, half), lambda i: (i, 0)),
            pl.BlockSpec((tm, half), lambda i: (i, 0)),
            pl.BlockSpec((tm, D), lambda i: (i, 0)),
            pl.BlockSpec((half, D), lambda i: (0, 0)),
            pl.BlockSpec((half, D), lambda i: (0, 0)),
            pl.BlockSpec((1, D), lambda i: (0, 0)),
        ],
        out_specs=pl.BlockSpec((tm, D), lambda i: (i, 0)),
        out_shape=jax.ShapeDtypeStruct((T, D), F32),
        compiler_params=_params(1),
        name="outproj",
    )(o, p, x2d, w_o, w_p, g)


def _rope_tables(pos):
    half = HEAD_DIM // 2
    inv = ROPE_THETA ** (-jnp.arange(half, dtype=F32) * 2.0 / HEAD_DIM)
    ang = pos.astype(F32)[:, None] * inv[None, :]
    reps = LANES // half
    cos = jnp.tile(jnp.cos(ang), (1, reps))
    sin = jnp.tile(jnp.sin(ang), (1, reps))
    lane = jnp.arange(LANES)
    sign = jnp.where((lane % HEAD_DIM) < half, -1.0, 1.0).astype(F32)
    return cos, sin * sign[None, :]


def kernel(x_prompt, x_sample, cache_k, cache_v, state_pool, page_table, norm_pre, norm_post, w_in,
           lambda_q1, lambda_k1, lambda_q2, lambda_k2, subln_g, pool_w, pool_scale, w_out):
    batch, seq, d_model = x_prompt.shape
    dec_batch, dec_seq, _ = x_sample.shape
    depth = w_in.shape[0]
    n_heads = cache_k.shape[3]
    attn_w = n_heads * HEAD_W
    past_len = page_table.shape[1] * cache_k.shape[2]
    pool_width = pool_scale.shape[1]

    cos_p, sin_p = _rope_tables(jnp.arange(seq))
    cos_s, sin_s = _rope_tables(past_len + jnp.arange(dec_seq))
    cos_s = jnp.tile(cos_s, (dec_batch, 1))
    sin_s = jnp.tile(sin_s, (dec_batch, 1))

    xp = x_prompt.reshape(batch * seq, d_model)
    xs = x_sample.reshape(dec_batch * dec_seq, d_model)
    outs = {k: [] for k in ("pp", "ps")}
    kv_p = kv_s = None

    for layer in range(depth):
        lam_init = _lambda_init(layer)
        w_o = w_out[layer, :attn_w].astype(BF16)
        w_p = w_out[layer, attn_w:].astype(BF16)
        pool_w_l = pool_w[layer].astype(BF16)
        g_pre = norm_pre[layer][None, :]
        g_post = norm_post[layer][None, :]
        g_sub = subln_g[layer][None, :]
        sc_l = pool_scale[layer][None, :]
        lamv = jnp.stack([lambda_q1[layer], lambda_k1[layer], lambda_q2[layer], lambda_k2[layer]])

        zp, kv_p = _inproj(xp, g_pre, w_in, cos_p, sin_p, kv_p, layer=layer, n_heads=n_heads,
                           tm=1024, tn=512)
        zs, kv_s = _inproj(xs, g_pre, w_in, cos_s, sin_s, kv_s, layer=layer, n_heads=n_heads,
                           tm=dec_batch * dec_seq, tn=512)
        op, os_ = _attention(zp, zs, cache_k, cache_v, page_table, lamv, g_sub, layer=layer,
                             batch=batch, seq=seq, t_new=dec_seq, n_heads=n_heads,
                             lam_init=lam_init, tq=512)

        zp3 = zp.reshape(batch, seq, zp.shape[1])
        pp, pp_state = _pool(zp3, None, pool_w_l, sc_l, layer=layer, nb=1, tt=512, start=0,
                             out_dtype=BF16)
        xp = _outproj(op, pp.reshape(batch * seq, pool_width), xp, w_o, w_p, g_post, tm=512)
        outs["pp"].append(pp_state)

        zs3 = zs.reshape(dec_batch, dec_seq, zs.shape[1])
        ps, ps_state = _pool(zs3, state_pool, pool_w_l, sc_l, layer=layer, nb=16, tt=dec_seq,
                             start=past_len, out_dtype=F32)
        xs = _outproj(os_, ps.reshape(dec_batch * dec_seq, pool_width), xs, w_o, w_p, g_post, tm=512)
        outs["ps"].append(ps_state)

    return (xp.reshape(batch, seq, d_model), xs.reshape(dec_batch, dec_seq, d_model),
            kv_p[0].reshape(depth, batch, seq, n_heads, HEAD_W),
            kv_p[1].reshape(depth, batch, seq, n_heads, HEAD_W), jnp.stack(outs["pp"]),
            kv_s[0].reshape(depth, dec_batch, dec_seq, n_heads, HEAD_W),
            kv_s[1].reshape(depth, dec_batch, dec_seq, n_heads, HEAD_W), jnp.stack(outs["ps"]))
```

```python
import functools
import math

import jax
import jax.numpy as jnp
from jax import lax
from jax.experimental import pallas as pl
from jax.experimental.pallas import tpu as pltpu

F32 = jnp.float32
BF16 = jnp.bfloat16

HEAD_DIM = 64
HEAD_W = 2 * HEAD_DIM
POOL_WINDOWS = (2, 4, 8, 16)
POOL_STATE = max(POOL_WINDOWS) - 1
POOL_PREV = POOL_STATE + 1
ROPE_THETA = 10000.0
EPS = 1e-6
NEG = -1e30
ATTN_SCALE = HEAD_DIM ** -0.5

LANES = 128
SUBLANES = 8
VMEM_LIMIT = 56 * 1024 * 1024


def _params(n_axes):
    return pltpu.CompilerParams(
        dimension_semantics=("arbitrary",) * n_axes, vmem_limit_bytes=VMEM_LIMIT)


def _lambda_init(layer):
    return 0.8 - 0.6 * math.exp(-0.3 * layer)


def _lam_from(lamv, lam_init):
    t1 = jnp.sum(lamv[0:1, :] * lamv[1:2, :], axis=-1, keepdims=True)
    t2 = jnp.sum(lamv[2:3, :] * lamv[3:4, :], axis=-1, keepdims=True)
    return jnp.exp(t1) - jnp.exp(t2) + lam_init


def _silu(x):
    return x * (1.0 / (1.0 + jnp.exp(-x)))


def _head_norm_gate(o, g, lam_init):
    ms = jnp.mean(o * o, axis=-1, keepdims=True)
    return (o * lax.rsqrt(ms + EPS) * g) * (1.0 - lam_init)


MXU_COLS = 256


def _inproj_kernel(x_ref, g_ref, w_ref, cos_ref, sin_ref, *rest, tiles_per_section, n_heads, aliased):
    z_ref, kf_ref, vf_ref, h_sc = rest[2:] if aliased else rest
    j = pl.program_id(1)
    tm, tn = z_ref.shape
    heads_per_tile = tn // HEAD_W

    @pl.when(j == 0)
    def _():
        x = x_ref[...]
        ms = jnp.mean(x * x, axis=-1, keepdims=True)
        h_sc[...] = ((x * lax.rsqrt(ms + EPS)) * g_ref[...]).astype(BF16)

    def rope(acc):
        cos = cos_ref[...]
        sin = sin_ref[...]
        lane = lax.broadcasted_iota(jnp.int32, cos.shape, 1)
        first_half = (lane % HEAD_DIM) < (HEAD_DIM // 2)
        out = []
        for hh in range(acc.shape[1] // LANES):
            blk = acc[:, hh * LANES:(hh + 1) * LANES]
            partner = jnp.where(first_half,
                                pltpu.roll(blk, LANES - HEAD_DIM // 2, 1),
                                pltpu.roll(blk, HEAD_DIM // 2, 1))
            out.append(blk * cos + partner * sin)
        return jnp.concatenate(out, axis=1)

    def tile(epilogue):
        for ch in range(tn // MXU_COLS):
            cols = slice(ch * MXU_COLS, (ch + 1) * MXU_COLS)
            epilogue(ch, cols, jnp.dot(h_sc[...], w_ref[:, cols].astype(BF16),
                                        preferred_element_type=F32))

    def head_major_store(ref, first_head, val):
        for hh in range(val.shape[1] // HEAD_W):
            ref[pl.ds(first_head + hh, tm, stride=n_heads), :] = val[:, hh * HEAD_W:(hh + 1) * HEAD_W]

    @pl.when(j < tiles_per_section)
    def _():
        def q_epilogue(ch, cols, acc):
            z_ref[:, cols] = rope(acc) * ATTN_SCALE
        tile(q_epilogue)

    for t in range(tiles_per_section):
        @pl.when(j == tiles_per_section + t)
        def _():
            def k_epilogue(ch, cols, acc):
                r = rope(acc)
                z_ref[:, cols] = r
                head_major_store(kf_ref, t * heads_per_tile + ch * (MXU_COLS // HEAD_W), r)
            tile(k_epilogue)

        @pl.when(j == 2 * tiles_per_section + t)
        def _():
            def v_epilogue(ch, cols, acc):
                z_ref[:, cols] = acc
                head_major_store(vf_ref, t * heads_per_tile + ch * (MXU_COLS // HEAD_W), acc)
            tile(v_epilogue)

    @pl.when(j >= 3 * tiles_per_section)
    def _():
        def plain_epilogue(ch, cols, acc):
            z_ref[:, cols] = acc
        tile(plain_epilogue)


def _inproj(x2d, g, w_in, cos_t, sin_t, kv_prev, *, layer, n_heads, tm, tn):
    T, D = x2d.shape
    depth, _, n_cols = w_in.shape
    period_blocks = cos_t.shape[0] // tm
    attn_w = n_cols // 6
    aliased = kv_prev is not None
    kern = functools.partial(_inproj_kernel, tiles_per_section=attn_w // tn, n_heads=n_heads,
                             aliased=aliased)
    kv_shape = jax.ShapeDtypeStruct((depth, T * n_heads, HEAD_W), F32)
    kv_spec = pl.BlockSpec((None, tm * n_heads, HEAD_W), lambda i, j: (layer, i, 0))
    in_specs = [
        pl.BlockSpec((tm, D), lambda i, j: (i, 0)),
        pl.BlockSpec((1, D), lambda i, j: (0, 0)),
        pl.BlockSpec((None, D, tn), lambda i, j: (layer, 0, j)),
        pl.BlockSpec((tm, LANES), lambda i, j: (i % period_blocks, 0)),
        pl.BlockSpec((tm, LANES), lambda i, j: (i % period_blocks, 0)),
    ]
    args = [x2d, g, w_in, cos_t, sin_t]
    if aliased:
        in_specs += [pl.BlockSpec(memory_space=pl.ANY)] * 2
        args += list(kv_prev)
    z, kf, vf = pl.pallas_call(
        kern,
        grid=(T // tm, n_cols // tn),
        in_specs=in_specs,
        out_specs=[pl.BlockSpec((tm, tn), lambda i, j: (i, j)), kv_spec, kv_spec],
        out_shape=[jax.ShapeDtypeStruct((T, n_cols), F32), kv_shape, kv_shape],
        scratch_shapes=[pltpu.VMEM((tm, D), BF16)],
        input_output_aliases={5: 1, 6: 2} if aliased else {},
        compiler_params=_params(2),
        name="inproj",
    )(*args)
    return z, (kf, vf)


ONES_ROWS = 16


def _prompt_attention(qi, q_ref, k_ref, v_ref, ga_ref, lamv_ref, g_ref, o_ref,
                      s_sc, p_sc, acc_sc, stat_sc, *, lam_init):
    tq = q_ref.shape[0]
    q = q_ref[...]
    lane = lax.broadcasted_iota(jnp.int32, q.shape, 1)
    q_maps = [jnp.where((lane >= c * HEAD_DIM) & (lane < (c + 1) * HEAD_DIM), q, 0.0).astype(BF16)
              for c in range(2)]
    nt = (((1,), (1,)), ((), ()))
    ones_rows = jnp.ones((ONES_ROWS, tq), BF16)

    def tile_rows(tile):
        return pl.ds(pl.multiple_of(tile * tq, tq), tq)

    def visit(i):
        return jnp.where(i == 0, qi, i - 1)

    def score_stage(i, slot):
        k = k_ref[tile_rows(visit(i)), :].astype(BF16)
        for c in range(2):
            s_sc[slot, c] = lax.dot_general(k, q_maps[c], nt, preferred_element_type=F32)

    def softmax_stage(slot, on_diag=False):
        for c in range(2):
            m_prev = stat_sc[c:c + 1, :]
            s_t = s_sc[slot, c]
            if on_diag:
                key = lax.broadcasted_iota(jnp.int32, s_t.shape, 0)
                qry = lax.broadcasted_iota(jnp.int32, s_t.shape, 1)
                s_t = jnp.where(key <= qry, s_t, NEG)
            m_new = jnp.maximum(m_prev, jnp.max(s_t, axis=0, keepdims=True))
            p_sc[c] = jnp.exp(s_t - m_new).astype(BF16)
            stat_sc[2 + c:3 + c, :] = jnp.exp(m_prev - m_new)
            stat_sc[c:c + 1, :] = m_new

    def value_stage(i):
        v_t = jnp.concatenate([v_ref[tile_rows(visit(i)), :].T.astype(BF16), ones_rows], axis=0)
        pv = [jnp.dot(v_t, p_sc[c], preferred_element_type=F32) for c in range(2)]
        for c in range(2):
            acc_sc[c] = stat_sc[2 + c:3 + c, :] * acc_sc[c] + pv[c]

    def step(i, slot):
        score_stage(i + 1, 1 - slot)
        value_stage(i - 1)
        softmax_stage(slot)

    def last_step(slot):
        value_stage(qi - 1)
        softmax_stage(slot)

    acc_sc[...] = jnp.zeros(acc_sc.shape, F32)
    stat_sc[...] = jnp.full(stat_sc.shape, NEG, F32)
    score_stage(0, 0)
    score_stage(jnp.minimum(1, qi), 1)
    softmax_stage(0, on_diag=True)

    @pl.loop(1, qi)
    def _(i):
        for slot in range(2):
            @pl.when(i % 2 == slot)
            def _():
                step(i, slot)

    for slot in range(2):
        @pl.when((qi >= 1) & (qi % 2 == slot))
        def _():
            last_step(slot)

    value_stage(qi)

    lam = _lam_from(lamv_ref[...], lam_init)
    acc0, acc1 = acc_sc[0], acc_sc[1]
    o_t = (acc0[:HEAD_W] / acc0[HEAD_W:HEAD_W + 1]
           - lam * (acc1[:HEAD_W] / acc1[HEAD_W:HEAD_W + 1]))
    o = _head_norm_gate(o_t.T, g_ref[...], lam_init)
    o_ref[...] = (o * _silu(ga_ref[...])).astype(o_ref.dtype)


def _decode_attention(q_ref, kn_ref, vn_ref, ga_ref, lamv_ref, g_ref, k_pages, v_pages, o_ref, *,
                      lam_init, n_heads):
    n_pages = len(k_pages)
    t_new = q_ref.shape[0]
    page = k_pages[0].shape[0] // n_heads
    half = n_heads // 2
    nt = (((1,), (1,)), ((), ()))

    def pair_rows(pages, h):
        return jnp.concatenate(
            [ref[pl.ds(h, 2 * page, stride=half), :] for ref in pages], axis=0).astype(BF16)

    lam = _lam_from(lamv_ref[...], lam_init)
    g = g_ref[...]
    n_rows = 4 * t_new
    row = lax.broadcasted_iota(jnp.int32, (n_rows, HEAD_W), 0)
    lane = lax.broadcasted_iota(jnp.int32, (n_rows, HEAD_W), 1)
    in_map = (lane // HEAD_DIM) == ((row // t_new) % 2)
    past = 2 * page * n_pages
    row_e = lax.broadcasted_iota(jnp.int32, (n_rows, past), 0) // (2 * t_new)
    own_past = (lax.broadcasted_iota(jnp.int32, (n_rows, past), 1) % 2) == row_e
    r_new = lax.broadcasted_iota(jnp.int32, (n_rows, 2 * t_new), 0)
    c_new = lax.broadcasted_iota(jnp.int32, (n_rows, 2 * t_new), 1)
    own_new = ((c_new // t_new) == (r_new // (2 * t_new))) & ((c_new % t_new) <= (r_new % t_new))

    def pair_cols(h):
        return [slice((h + e * half) * HEAD_W, (h + e * half + 1) * HEAD_W) for e in range(2)]

    def new_rows(ref, h):
        cols = pair_cols(h)
        return jnp.concatenate([ref[:, cols[0]], ref[:, cols[1]]], axis=0).astype(BF16)

    scores = []
    for h in range(half):
        cols = pair_cols(h)
        q_rep = jnp.concatenate([q_ref[:, cols[0]]] * 2 + [q_ref[:, cols[1]]] * 2, axis=0)
        q4 = jnp.where(in_map, q_rep, 0.0).astype(BF16)
        s_past = lax.dot_general(q4, pair_rows(k_pages, h), nt, preferred_element_type=F32)
        s_new = lax.dot_general(q4, new_rows(kn_ref, h), nt, preferred_element_type=F32)
        scores.append((s_past, s_new))

    weights = []
    for s_past, s_new in scores:
        s_past = jnp.where(own_past, s_past, NEG)
        s_new = jnp.where(own_new, s_new, NEG)
        m = jnp.maximum(jnp.max(s_past, axis=-1, keepdims=True), jnp.max(s_new, axis=-1, keepdims=True))
        p_past = jnp.exp(s_past - m)
        p_new = jnp.exp(s_new - m)
        l = jnp.sum(p_past, axis=-1, keepdims=True) + jnp.sum(p_new, axis=-1, keepdims=True)
        weights.append((p_past.astype(BF16), p_new.astype(BF16), l))

    outs = [None] * n_heads
    for h, (p_past, p_new, l) in enumerate(weights):
        o4 = (jnp.dot(p_past, pair_rows(v_pages, h), preferred_element_type=F32)
              + jnp.dot(p_new, new_rows(vn_ref, h), preferred_element_type=F32)) / l
        for e in range(2):
            r0 = e * 2 * t_new
            outs[h + e * half] = _head_norm_gate(
                o4[r0:r0 + t_new] - lam * o4[r0 + t_new:r0 + 2 * t_new], g, lam_init)
    o_ref[...] = jnp.concatenate(outs, axis=1) * _silu(ga_ref[...])


def _attention_kernel(pt_ref, qp_ref, kp_ref, vp_ref, gap_ref, qs_ref, kns_ref, vns_ref, gas_ref,
                      lamv_ref, g_ref, *rest, lam_init, n_pages, n_heads):
    del pt_ref
    k_pages = rest[:n_pages]
    v_pages = rest[n_pages:2 * n_pages]
    op_ref, os_ref = rest[2 * n_pages:2 * n_pages + 2]
    scratch = rest[2 * n_pages + 2:]
    _decode_attention(qs_ref, kns_ref, vns_ref, gas_ref, lamv_ref, g_ref, k_pages, v_pages, os_ref,
                      lam_init=lam_init, n_heads=n_heads)
    _prompt_attention(pl.program_id(2), qp_ref, kp_ref, vp_ref, gap_ref, lamv_ref, g_ref, op_ref,
                      *scratch, lam_init=lam_init)


def _attention(zp, zs, cache_k, cache_v, page_table, lamv, subln_g, *, layer, batch, seq, t_new,
               n_heads, lam_init, tq):
    nq = seq // tq
    dec_batch, n_pages = page_table.shape
    assert dec_batch == batch * n_heads * nq, (dec_batch, batch, n_heads, nq)
    depth, n_phys, page = cache_k.shape[:3]
    attn_w = n_heads * HEAD_W
    kb = attn_w // LANES
    cache_k = cache_k.reshape(depth, n_phys, page * n_heads, HEAD_W)
    cache_v = cache_v.reshape(depth, n_phys, page * n_heads, HEAD_W)

    def seq_of(b, h, i):
        return (b * n_heads + h) * nq + i

    row_spec = lambda sec: pl.BlockSpec((t_new, attn_w), lambda b, h, i, pt: (seq_of(b, h, i), sec))
    page_specs = [
        pl.BlockSpec((None, None, page * n_heads, HEAD_W),
                     lambda b, h, i, pt, j=j: (layer, pt[seq_of(b, h, i), j], 0, 0))
        for j in range(n_pages)
    ]
    kern = functools.partial(_attention_kernel, lam_init=lam_init, n_pages=n_pages, n_heads=n_heads)
    grid_spec = pltpu.PrefetchScalarGridSpec(
        num_scalar_prefetch=1,
        grid=(batch, n_heads, nq),
        in_specs=[
            pl.BlockSpec((tq, LANES), lambda b, h, i, pt: (b * nq + i, h)),
            pl.BlockSpec((seq, LANES), lambda b, h, i, pt: (b, kb + h)),
            pl.BlockSpec((seq, LANES), lambda b, h, i, pt: (b, 2 * kb + h)),
            pl.BlockSpec((tq, LANES), lambda b, h, i, pt: (b * nq + i, 3 * kb + h)),
            row_spec(0), row_spec(1), row_spec(2), row_spec(3),
            pl.BlockSpec((4, HEAD_DIM), lambda b, h, i, pt: (0, 0)),
            pl.BlockSpec((1, HEAD_W), lambda b, h, i, pt: (0, 0)),
        ] + page_specs + page_specs,
        out_specs=[
            pl.BlockSpec((tq, LANES), lambda b, h, i, pt: (b * nq + i, h)),
            pl.BlockSpec((t_new, attn_w), lambda b, h, i, pt: (seq_of(b, h, i), 0)),
        ],
        scratch_shapes=[pltpu.VMEM((2, 2, tq, tq), F32), pltpu.VMEM((2, tq, tq), BF16),
                        pltpu.VMEM((2, HEAD_W + ONES_ROWS, tq), F32),
                        pltpu.VMEM((SUBLANES, tq), F32)],
    )
    return pl.pallas_call(
        kern,
        grid_spec=grid_spec,
        out_shape=[jax.ShapeDtypeStruct((batch * seq, attn_w), BF16),
                   jax.ShapeDtypeStruct((dec_batch * t_new, attn_w), F32)],
        compiler_params=_params(3),
        name="attention",
    )(page_table, zp, zp, zp, zp, zs, zs, zs, zs, lamv, subln_g,
      *([cache_k] * n_pages), *([cache_v] * n_pages))


def _pool_kernel(u_ref, prev_ref, gp_ref, w_ref, sc_ref, p_ref, state_ref, ext_sc, *,
                 start, zero_first_prefix):
    i = pl.program_id(1)
    nb, tt, width = u_ref.shape
    gw = width // len(POOL_WINDOWS)
    n_prev = prev_ref.shape[1]
    prev = prev_ref[...]
    if zero_first_prefix:
        prev = jnp.where(i == 0, 0.0, prev)
    u = u_ref[...]
    ext_sc[:, POOL_PREV - n_prev:POOL_PREV, :] = prev
    ext_sc[:, POOL_PREV:POOL_PREV + tt, :] = u
    pos = start + i * tt + lax.broadcasted_iota(jnp.int32, (1, tt, 1), 1)
    for g, w in enumerate(POOL_WINDOWS):
        cols = slice(g * gw, (g + 1) * gw)
        ssum = u[:, :, cols]
        for back in range(1, w):
            ssum = ssum + ext_sc[:, POOL_PREV - back:POOL_PREV - back + tt, cols]
        cnt = jnp.minimum(pos + 1, w).astype(F32)
        d = (ssum / cnt - u[:, :, cols]).astype(BF16).reshape(nb * tt, gw)
        y = jnp.dot(d, w_ref[g], preferred_element_type=F32) * sc_ref[:, cols]
        y = y * _silu(gp_ref[:, :, cols].reshape(nb * tt, gw))
        p_ref[:, :, cols] = y.reshape(nb, tt, gw).astype(p_ref.dtype)

    @pl.when(i == pl.num_programs(1) - 1)
    def _():
        state_ref[...] = ext_sc[:, tt + POOL_PREV - POOL_STATE:tt + POOL_PREV, :]


def _pool(z3, state, pool_w_bf16, pool_scale, *, layer, nb, tt, start, out_dtype):
    n_seq, seq, n_cols = z3.shape
    width = n_cols // 6
    u_blk, gp_blk = 4, 5
    if state is None:
        per = tt // POOL_PREV
        prev_src = z3
        prev_spec = pl.BlockSpec((nb, POOL_PREV, width),
                                 lambda b, i: (b, jnp.maximum(i * per - 1, 0), u_blk))
    else:
        prev_src = state
        prev_spec = pl.BlockSpec((None, nb, POOL_STATE, width), lambda b, i: (layer, b, 0, 0))
    kern = functools.partial(_pool_kernel, start=start, zero_first_prefix=state is None)
    return pl.pallas_call(
        kern,
        grid=(n_seq // nb, seq // tt),
        in_specs=[
            pl.BlockSpec((nb, tt, width), lambda b, i: (b, i, u_blk)),
            prev_spec,
            pl.BlockSpec((nb, tt, width), lambda b, i: (b, i, gp_blk)),
            pl.BlockSpec(pool_w_bf16.shape, lambda b, i: (0, 0, 0)),
            pl.BlockSpec((1, width), lambda b, i: (0, 0)),
        ],
        out_specs=[pl.BlockSpec((nb, tt, width), lambda b, i: (b, i, 0)),
                   pl.BlockSpec((nb, POOL_STATE, width), lambda b, i: (b, 0, 0))],
        out_shape=[jax.ShapeDtypeStruct((n_seq, seq, width), out_dtype),
                   jax.ShapeDtypeStruct((n_seq, POOL_STATE, width), F32)],
        scratch_shapes=[pltpu.VMEM((nb, POOL_PREV + tt, width), F32)],
        compiler_params=_params(2),
        name="pool",
    )(z3, prev_src, z3, pool_w_bf16, pool_scale)


def _outproj_kernel(o_ref, p_ref, x_ref, wo_ref, wp_ref, g_ref, out_ref):
    y = (jnp.dot(o_ref[...].astype(BF16), wo_ref[...], preferred_element_type=F32)
         + jnp.dot(p_ref[...].astype(BF16), wp_ref[...], preferred_element_type=F32))
    ms = jnp.mean(y * y, axis=-1, keepdims=True)
    out_ref[...] = x_ref[...] + (y * lax.rsqrt(ms + EPS)) * g_ref[...]


def _outproj(o, p, x2d, w_o, w_p, g, *, tm):
    T, D = x2d.shape
    half = o.shape[1]
    return pl.pallas_call(
        _outproj_kernel,
        grid=(T // tm,),
        in_specs=[
            pl.BlockSpec((tm, half), lambda i: (i, 0)),
            pl.BlockSpec((tm, half), lambda i: (i, 0)),
            pl.BlockSpec((tm, D), lambda i: (i, 0)),
            pl.BlockSpec((half, D), lambda i: (0, 0)),
            pl.BlockSpec((half, D), lambda i: (0, 0)),
            pl.BlockSpec((1, D), lambda i: (0, 0)),
        ],
        out_specs=pl.BlockSpec((tm, D), lambda i: (i, 0)),
        out_shape=jax.ShapeDtypeStruct((T, D), F32),
        compiler_params=_params(1),
        name="outproj",
    )(o, p, x2d, w_o, w_p, g)


def _rope_tables(pos):
    half = HEAD_DIM // 2
    inv = ROPE_THETA ** (-jnp.arange(half, dtype=F32) * 2.0 / HEAD_DIM)
    ang = pos.astype(F32)[:, None] * inv[None, :]
    reps = LANES // half
    cos = jnp.tile(jnp.cos(ang), (1, reps))
    sin = jnp.tile(jnp.sin(ang), (1, reps))
    lane = jnp.arange(LANES)
    sign = jnp.where((lane % HEAD_DIM) < half, -1.0, 1.0).astype(F32)
    return cos, sin * sign[None, :]


def kernel(x_prompt, x_sample, cache_k, cache_v, state_pool, page_table, norm_pre, norm_post, w_in,
           lambda_q1, lambda_k1, lambda_q2, lambda_k2, subln_g, pool_w, pool_scale, w_out):
    batch, seq, d_model = x_prompt.shape
    dec_batch, dec_seq, _ = x_sample.shape
    depth = w_in.shape[0]
    n_heads = cache_k.shape[3]
    attn_w = n_heads * HEAD_W
    past_len = page_table.shape[1] * cache_k.shape[2]
    pool_width = pool_scale.shape[1]

    cos_p, sin_p = _rope_tables(jnp.arange(seq))
    cos_s, sin_s = _rope_tables(past_len + jnp.arange(dec_seq))
    cos_s = jnp.tile(cos_s, (dec_batch, 1))
    sin_s = jnp.tile(sin_s, (dec_batch, 1))

    xp = x_prompt.reshape(batch * seq, d_model)
    xs = x_sample.reshape(dec_batch * dec_seq, d_model)
    outs = {k: [] for k in ("pp", "ps")}
    kv_p = kv_s = None

    for layer in range(depth):
        lam_init = _lambda_init(layer)
        w_o = w_out[layer, :attn_w].astype(BF16)
        w_p = w_out[layer, attn_w:].astype(BF16)
        pool_w_l = pool_w[layer].astype(BF16)
        g_pre = norm_pre[layer][None, :]
        g_post = norm_post[layer][None, :]
        g_sub = subln_g[layer][None, :]
        sc_l = pool_scale[layer][None, :]
        lamv = jnp.stack([lambda_q1[layer], lambda_k1[layer], lambda_q2[layer], lambda_k2[layer]])

        zp, kv_p = _inproj(xp, g_pre, w_in, cos_p, sin_p, kv_p, layer=layer, n_heads=n_heads,
                           tm=1024, tn=512)
        zs, kv_s = _inproj(xs, g_pre, w_in, cos_s, sin_s, kv_s, layer=layer, n_heads=n_heads,
                           tm=dec_batch * dec_seq, tn=512)
        op, os_ = _attention(zp, zs, cache_k, cache_v, page_table, lamv, g_sub, layer=layer,
                             batch=batch, seq=seq, t_new=dec_seq, n_heads=n_heads,
                             lam_init=lam_init, tq=512)

        zp3 = zp.reshape(batch, seq, zp.shape[1])
        pp, pp_state = _pool(zp3, None, pool_w_l, sc_l, layer=layer, nb=1, tt=512, start=0,
                             out_dtype=BF16)
        xp = _outproj(op, pp.reshape(batch * seq, pool_width), xp, w_o, w_p, g_post, tm=512)
        outs["pp"].append(pp_state)

        zs3 = zs.reshape(dec_batch, dec_seq, zs.shape[1])
        ps, ps_state = _pool(zs3, state_pool, pool_w_l, sc_l, layer=layer, nb=16, tt=dec_seq,
                             start=past_len, out_dtype=F32)
        xs = _outproj(os_, ps.reshape(dec_batch * dec_seq, pool_width), xs, w_o, w_p, g_post, tm=512)
        outs["ps"].append(ps_state)

    return (xp.reshape(batch, seq, d_model), xs.reshape(dec_batch, dec_seq, d_model),
            kv_p[0].reshape(depth, batch, seq, n_heads, HEAD_W),
            kv_p[1].reshape(depth, batch, seq, n_heads, HEAD_W), jnp.stack(outs["pp"]),
            kv_s[0].reshape(depth, dec_batch, dec_seq, n_heads, HEAD_W),
            kv_s[1].reshape(depth, dec_batch, dec_seq, n_heads, HEAD_W), jnp.stack(outs["ps"]))
```
